```python
import math, functools
import jax, jax.numpy as jnp
from jax import lax
import numpy as np

D_MODEL = 4096
BATCH = 4
SEQ = 2048
DEPTH = 2
DEC_BATCH = 8
DEC_SEQ = 4
PAST_LEN = 16384
PAGE_SIZE = 128

DA_HEADS = 8
DA_QK = 128
DA_V = 2 * DA_QK
DA_WIDTH = DA_HEADS * DA_V
Q_COLS = DA_HEADS * 2 * DA_QK
SC_WIDTH = D_MODEL // 4
CONV_W = 3
N_MEM = 256
MEM_HEADS = 4
MEM_HD = D_MODEL // 16
MEM_WIDTH = MEM_HEADS * MEM_HD
MIX_WIDTH = DA_WIDTH + SC_WIDTH + MEM_WIDTH
IN_COLS = 2 * Q_COLS + DA_WIDTH + 3 * SC_WIDTH + MEM_WIDTH
D_FF = ((8 * D_MODEL // 3 + 255) // 256) * 256
ROPE_THETA = 10000.0
RMS_EPS = 1e-6
Q_BLOCK = 128
NEG_INF = -1e30

kernel_name = "hybrid_diffattn_shortconv_memxattn_convffn_step"


def _rms_norm(x, g):
    xf = x.astype(jnp.float32)
    y = xf * lax.rsqrt(jnp.mean(xf * xf, axis=-1, keepdims=True) + RMS_EPS)
    return (y * g.astype(jnp.float32)).astype(x.dtype)


def _rope(x, pos):
    half = x.shape[-1] // 2
    inv = ROPE_THETA ** (-jnp.arange(half, dtype=jnp.float32) / half)
    ang = pos.astype(jnp.float32)[:, None] * inv[None, :]
    ang = ang.reshape((ang.shape[0],) + (1,) * (x.ndim - 3) + (half,))
    cos, sin = jnp.cos(ang), jnp.sin(ang)
    xf = x.astype(jnp.float32)
    x1, x2 = xf[..., :half], xf[..., half:]
    return jnp.concatenate([x1 * cos - x2 * sin, x2 * cos + x1 * sin], axis=-1).astype(x.dtype)


def _causal_dwconv(x, hist, w, b=None):
    xp = jnp.concatenate([hist, x], axis=1)
    t = x.shape[1]
    y = w[0] * xp[:, 0:t]
    for j in range(1, CONV_W):
        y = y + w[j] * xp[:, j:j + t]
    if b is not None:
        y = y + b
    return y, xp[:, -(CONV_W - 1):]


def _lambda_init(l):
    return 0.8 - 0.6 * math.exp(-0.3 * l)


def _diff_attn_prompt(q, k, v):
    b, s = q.shape[:2]
    nb = s // Q_BLOCK
    scale = DA_QK ** -0.5
    qb = jnp.swapaxes(q.reshape(b, nb, Q_BLOCK, DA_HEADS, 2, DA_QK), 0, 1)
    kf = k.astype(jnp.float32)
    vf = v.astype(jnp.float32)
    kpos = jnp.arange(s)

    def block(args):
        qi, i = args
        sc = jnp.einsum("bqhmd,bkhmd->bmhqk", qi.astype(jnp.float32) * scale, kf)
        qpos = i * Q_BLOCK + jnp.arange(Q_BLOCK)
        sc = jnp.where(kpos[None, :] <= qpos[:, None], sc, NEG_INF)
        p = jax.nn.softmax(sc, axis=-1)
        return jnp.einsum("bmhqk,bkhd->bqmhd", p, vf)

    o = lax.map(block, (qb, jnp.arange(nb)))
    return jnp.swapaxes(o, 0, 1).reshape(b, s, 2, DA_HEADS, DA_V)


def _online_update(carry, sc, v):
    m, ls, acc = carry
    m_new = jnp.maximum(m, sc.max(-1))
    corr = jnp.exp(m - m_new)
    p = jnp.exp(sc - m_new[..., None])
    acc = acc * corr[..., None] + jnp.einsum("bmhqk,bkhd->bmhqd", p, v)
    return (m_new, ls * corr + p.sum(-1), acc)


def _diff_attn_sample(q, k_new, v_new, cache_k, cache_v, page_table, l):
    bd, t = q.shape[:2]
    qf = q.astype(jnp.float32) * (DA_QK ** -0.5)
    init = (jnp.full((bd, 2, DA_HEADS, t), NEG_INF, jnp.float32),
            jnp.zeros((bd, 2, DA_HEADS, t), jnp.float32),
            jnp.zeros((bd, 2, DA_HEADS, t, DA_V), jnp.float32))

    def page_step(carry, pcol):
        kp = cache_k[l, pcol].astype(jnp.float32)
        vp = cache_v[l, pcol].astype(jnp.float32)
        sc = jnp.einsum("bqhmd,bkhmd->bmhqk", qf, kp)
        return _online_update(carry, sc, vp), None

    carry, _ = lax.scan(page_step, init, page_table.T)
    sc_new = jnp.einsum("bqhmd,bkhmd->bmhqk", qf, k_new.astype(jnp.float32))
    causal = jnp.tril(jnp.ones((t, t), dtype=bool))
    sc_new = jnp.where(causal, sc_new, NEG_INF)
    m, ls, acc = _online_update(carry, sc_new, v_new.astype(jnp.float32))
    o = acc / ls[..., None]
    return jnp.transpose(o, (0, 3, 1, 2, 4))


def _mem_kv(mem, l, p):
    b, n, _ = mem.shape
    hm = _rms_norm(mem, p["norm_mem"][l])
    mk = _rms_norm((hm @ p["w_mem_k"][l]).reshape(b, n, MEM_HEADS, MEM_HD), p["mk_norm"][l])
    mv = (hm @ p["w_mem_v"][l]).reshape(b, n, MEM_HEADS, MEM_HD)
    return mk, mv


def _mem_attn(q, mk, mv):
    sc = jnp.einsum("bqhd,bkhd->bhqk", q.astype(jnp.float32) * (MEM_HD ** -0.5), mk.astype(jnp.float32))
    pr = jax.nn.softmax(sc, axis=-1)
    return jnp.einsum("bhqk,bkhd->bqhd", pr, mv.astype(jnp.float32))


def _layer(x, pos, l, p, attend, conv_hist, ffn_hist, mem_k, mem_v):
    b, t, _ = x.shape
    h = _rms_norm(x, p["norm_mix"][l])
    proj = h @ p["w_in"][l]
    c0 = Q_COLS
    c1 = c0 + Q_COLS
    c2 = c1 + DA_WIDTH
    c3 = c2 + SC_WIDTH
    c4 = c3 + SC_WIDTH
    c5 = c4 + SC_WIDTH
    q, k, v, gb, gc, hc, mq = jnp.split(proj, [c0, c1, c2, c3, c4, c5], axis=-1)

    q = _rope(_rms_norm(q.reshape(b, t, DA_HEADS, 2, DA_QK), p["q_norm"][l]), pos)
    k = _rope(_rms_norm(k.reshape(b, t, DA_HEADS, 2, DA_QK), p["k_norm"][l]), pos)
    v = v.reshape(b, t, DA_HEADS, DA_V)
    o = attend(q, k, v)
    lam_init = _lambda_init(l)
    f32 = jnp.float32
    lam = (jnp.exp(jnp.sum(p["lambda_q1"][l].astype(f32) * p["lambda_k1"][l].astype(f32)))
           - jnp.exp(jnp.sum(p["lambda_q2"][l].astype(f32) * p["lambda_k2"][l].astype(f32))) + lam_init)
    da = o[:, :, 0] - lam * o[:, :, 1]
    da = (_rms_norm(da, p["subln"][l]) * (1.0 - lam_init)).reshape(b, t, DA_WIDTH).astype(x.dtype)

    cu, conv_state = _causal_dwconv(gc * hc, conv_hist, p["sc_conv_w"][l])
    sc_out = gb * cu

    mq = _rms_norm(mq.reshape(b, t, MEM_HEADS, MEM_HD), p["mq_norm"][l])
    mo = _mem_attn(mq, mem_k, mem_v).reshape(b, t, MEM_WIDTH).astype(x.dtype)

    x = x + jnp.concatenate([da, sc_out, mo], axis=-1) @ p["w_out"][l]

    h2 = _rms_norm(x, p["norm_ffn"][l])
    g = h2 @ p["w_gate"][l]
    u = h2 @ p["w_up"][l]
    gconv, ffn_state = _causal_dwconv(g, ffn_hist, p["ffn_conv_w"][l], p["ffn_conv_b"][l])
    x = x + (jax.nn.silu(gconv) * u) @ p["w_down"][l]
    return x, k, v, conv_state, ffn_state


def setup_inputs(seed: int = 0) -> dict:
    key = jax.random.key(seed)
    ks = jax.random.split(key, 40)
    f32 = jnp.float32
    n_pages = PAST_LEN // PAGE_SIZE
    n_used = DEC_BATCH * n_pages
    n_pool = (5 * n_used + 3) // 4

    def nrm(i, shape, scale=1.0):
        return jax.random.normal(ks[i], shape, f32) * scale

    def gain(i, shape):
        return 1.0 + 0.02 * jax.random.normal(ks[i], shape, f32)

    page_table = jax.random.permutation(ks[0], n_pool)[:n_used].reshape(DEC_BATCH, n_pages).astype(jnp.int32)
    return {
        "x_prompt": nrm(1, (BATCH, SEQ, D_MODEL)),
        "x_sample": nrm(2, (DEC_BATCH, DEC_SEQ, D_MODEL)),
        "cache_k": nrm(3, (DEPTH, n_pool, PAGE_SIZE, DA_HEADS, 2, DA_QK)),
        "cache_v": nrm(4, (DEPTH, n_pool, PAGE_SIZE, DA_HEADS, DA_V)),
        "cache_mem_k": nrm(5, (DEPTH, DEC_BATCH, N_MEM, MEM_HEADS, MEM_HD)),
        "cache_mem_v": nrm(6, (DEPTH, DEC_BATCH, N_MEM, MEM_HEADS, MEM_HD)),
        "state_conv": nrm(7, (DEPTH, DEC_BATCH, CONV_W - 1, SC_WIDTH)),
        "state_ffn": nrm(8, (DEPTH, DEC_BATCH, CONV_W - 1, D_FF)),
        "page_table": page_table,
        "mem_prompt": nrm(9, (BATCH, N_MEM, D_MODEL)),
        "norm_mix": gain(10, (DEPTH, D_MODEL)),
        "w_in": nrm(11, (DEPTH, D_MODEL, IN_COLS), D_MODEL ** -0.5),
        "q_norm": gain(12, (DEPTH, DA_QK)),
        "k_norm": gain(13, (DEPTH, DA_QK)),
        "lambda_q1": nrm(14, (DEPTH, DA_QK), 0.1),
        "lambda_k1": nrm(15, (DEPTH, DA_QK), 0.1),
        "lambda_q2": nrm(16, (DEPTH, DA_QK), 0.1),
        "lambda_k2": nrm(17, (DEPTH, DA_QK), 0.1),
        "subln": gain(18, (DEPTH, DA_V)),
        "sc_conv_w": nrm(19, (DEPTH, CONV_W, SC_WIDTH), CONV_W ** -0.5),
        "mq_norm": gain(20, (DEPTH, MEM_HD)),
        "norm_mem": gain(21, (DEPTH, D_MODEL)),
        "w_mem_k": nrm(22, (DEPTH, D_MODEL, MEM_WIDTH), D_MODEL ** -0.5),
        "w_mem_v": nrm(23, (DEPTH, D_MODEL, MEM_WIDTH), D_MODEL ** -0.5),
        "mk_norm": gain(24, (DEPTH, MEM_HD)),
        "w_out": nrm(25, (DEPTH, MIX_WIDTH, D_MODEL), MIX_WIDTH ** -0.5),
        "norm_ffn": gain(26, (DEPTH, D_MODEL)),
        "w_gate": nrm(27, (DEPTH, D_MODEL, D_FF), D_MODEL ** -0.5),
        "w_up": nrm(28, (DEPTH, D_MODEL, D_FF), D_MODEL ** -0.5),
        "ffn_conv_w": nrm(29, (DEPTH, CONV_W, D_FF), CONV_W ** -0.5),
        "ffn_conv_b": nrm(30, (DEPTH, D_FF), 0.01),
        "w_down": nrm(31, (DEPTH, D_FF, D_MODEL), D_FF ** -0.5),
    }


def reference(x_prompt, x_sample, cache_k, cache_v, cache_mem_k, cache_mem_v, state_conv, state_ffn,
              page_table, mem_prompt, norm_mix, w_in, q_norm, k_norm, lambda_q1, lambda_k1, lambda_q2,
              lambda_k2, subln, sc_conv_w, mq_norm, norm_mem, w_mem_k, w_mem_v, mk_norm, w_out, norm_ffn,
              w_gate, w_up, ffn_conv_w, ffn_conv_b, w_down):
    p = {"norm_mix": norm_mix, "w_in": w_in, "q_norm": q_norm, "k_norm": k_norm,
         "lambda_q1": lambda_q1, "lambda_k1": lambda_k1, "lambda_q2": lambda_q2, "lambda_k2": lambda_k2,
         "subln": subln, "sc_conv_w": sc_conv_w, "mq_norm": mq_norm, "norm_mem": norm_mem,
         "w_mem_k": w_mem_k, "w_mem_v": w_mem_v, "mk_norm": mk_norm, "w_out": w_out,
         "norm_ffn": norm_ffn, "w_gate": w_gate, "w_up": w_up, "ffn_conv_w": ffn_conv_w,
         "ffn_conv_b": ffn_conv_b, "w_down": w_down}
    b, s, _ = x_prompt.shape
    bd, t, _ = x_sample.shape
    pos_prompt = jnp.arange(s, dtype=jnp.int32)
    pos_sample = PAST_LEN + jnp.arange(t, dtype=jnp.int32)

    xp, xs = x_prompt, x_sample
    kp_l, vp_l, cp_l, fp_l, mkp_l, mvp_l = [], [], [], [], [], []
    ks_l, vs_l, cs_l, fs_l = [], [], [], []
    for l in range(DEPTH):
        mk, mv = _mem_kv(mem_prompt, l, p)
        xp, k_new, v_new, c_new, f_new = _layer(
            xp, pos_prompt, l, p, _diff_attn_prompt,
            jnp.zeros((b, CONV_W - 1, SC_WIDTH), xp.dtype),
            jnp.zeros((b, CONV_W - 1, D_FF), xp.dtype), mk, mv)
        kp_l.append(k_new); vp_l.append(v_new); cp_l.append(c_new); fp_l.append(f_new)
        mkp_l.append(mk); mvp_l.append(mv)

        attend = functools.partial(_diff_attn_sample, cache_k=cache_k, cache_v=cache_v,
                                   page_table=page_table, l=l)
        xs, k_new, v_new, c_new, f_new = _layer(
            xs, pos_sample, l, p, attend, state_conv[l], state_ffn[l], cache_mem_k[l], cache_mem_v[l])
        ks_l.append(k_new); vs_l.append(v_new); cs_l.append(c_new); fs_l.append(f_new)

    return (xp, xs,
            jnp.stack(kp_l), jnp.stack(vp_l), jnp.stack(cp_l), jnp.stack(fp_l),
            jnp.stack(mkp_l), jnp.stack(mvp_l),
            jnp.stack(ks_l), jnp.stack(vs_l), jnp.stack(cs_l), jnp.stack(fs_l))
```

```python
import functools
import math

import jax
import jax.numpy as jnp
from jax import lax
from jax.experimental import pallas as pl
from jax.experimental.pallas import tpu as pltpu

ROPE_THETA = 10000.0
RMS_EPS = 1e-6
NEG_INF = -1e30
CONV_W = 3
HIST = CONV_W - 1
V7X_VMEM_BYTES = 64 * 1024 * 1024
VMEM_LIMIT = V7X_VMEM_BYTES - 8 * 1024 * 1024
SUBLANES = 8
LANES = 128

F32 = jnp.float32
BF16 = jnp.bfloat16


def _params(*sem):
    return pltpu.CompilerParams(dimension_semantics=sem, vmem_limit_bytes=VMEM_LIMIT)


def _pick(n, pref, align):
    if n <= pref:
        return n
    t = (pref // align) * align
    while t >= align:
        if n % t == 0:
            return t
        t -= align
    return n


def _lambda_init(l):
    return 0.8 - 0.6 * math.exp(-0.3 * l)


def _rmsnorm_kernel(x_ref, g_ref, o_ref):
    x = x_ref[...]
    ms = jnp.mean(x * x, axis=-1, keepdims=True)
    o_ref[...] = (x * lax.rsqrt(ms + RMS_EPS) * g_ref[...]).astype(o_ref.dtype)


def _rmsnorm(x, g):
    m, d = x.shape
    tm = _pick(m, 256, SUBLANES)
    return pl.pallas_call(
        _rmsnorm_kernel,
        grid=(m // tm,),
        in_specs=[pl.BlockSpec((tm, d), lambda i: (i, 0)),
                  pl.BlockSpec((1, d), lambda i: (0, 0))],
        out_specs=pl.BlockSpec((tm, d), lambda i: (i, 0)),
        out_shape=jax.ShapeDtypeStruct((m, d), BF16),
        compiler_params=_params("parallel"),
    )(x, g.reshape(1, d))


def _matmul_parts_kernel(*refs, nparts):
    xs, ws = refs[:nparts], refs[nparts:2 * nparts]
    r_ref, o_ref = refs[2 * nparts], refs[2 * nparts + 1]
    acc = r_ref[...]
    for x_ref, w_ref in zip(xs, ws):
        acc = acc + jnp.dot(x_ref[...], w_ref[...], preferred_element_type=F32)
    o_ref[...] = acc


def _matmul_parts(xs, w, res, *, tm=1024, tn=1024):
    m = xs[0].shape[0]
    n = w.shape[1]
    tm = _pick(m, tm, SUBLANES)
    tn = _pick(n, tn, LANES)
    in_specs, off = [], 0
    for x in xs:
        in_specs.append(pl.BlockSpec((tm, x.shape[1]), lambda i, j: (i, 0)))
    for x in xs:
        kp = x.shape[1]
        assert off % kp == 0
        in_specs.append(pl.BlockSpec((kp, tn), lambda i, j, r=off // kp: (r, j)))
        off += kp
    assert off == w.shape[0]
    in_specs.append(pl.BlockSpec((tm, tn), lambda i, j: (i, j)))
    return pl.pallas_call(
        functools.partial(_matmul_parts_kernel, nparts=len(xs)),
        grid=(m // tm, n // tn),
        in_specs=in_specs,
        out_specs=pl.BlockSpec((tm, tn), lambda i, j: (i, j)),
        out_shape=jax.ShapeDtypeStruct((m, n), F32),
        compiler_params=_params("parallel", "parallel"),
    )(*xs, *([w] * len(xs)), res)


def _matmul_kernel(*refs, nk, has_res):
    if has_res:
        x_ref, w_ref, r_ref, o_ref = refs
    else:
        x_ref, w_ref, o_ref = refs
        r_ref = None
    d = jnp.dot(x_ref[...], w_ref[...], preferred_element_type=F32)
    if nk == 1:
        o_ref[...] = r_ref[...] + d if has_res else d
    else:
        k = pl.program_id(2)

        @pl.when(k == 0)
        def _():
            o_ref[...] = r_ref[...] + d if has_res else d

        @pl.when(k > 0)
        def _():
            o_ref[...] += d


def _matmul(x, w, res=None, *, tm=1024, tn=1024, tk=None):
    m, kdim = x.shape
    n = w.shape[1]
    tm = _pick(m, tm, SUBLANES)
    tn = _pick(n, tn, LANES)
    tk = kdim if tk is None else tk
    nk = kdim // tk
    in_specs = [pl.BlockSpec((tm, tk), lambda i, j, k: (i, k)),
                pl.BlockSpec((tk, tn), lambda i, j, k: (k, j))]
    args = [x, w]
    if res is not None:
        in_specs.append(pl.BlockSpec((tm, tn), lambda i, j, k: (i, j)))
        args.append(res)
    return pl.pallas_call(
        functools.partial(_matmul_kernel, nk=nk, has_res=res is not None),
        grid=(m // tm, n // tn, nk),
        in_specs=in_specs,
        out_specs=pl.BlockSpec((tm, tn), lambda i, j, k: (i, j)),
        out_shape=jax.ShapeDtypeStruct((m, n), F32),
        compiler_params=_params("parallel", "parallel", "arbitrary"),
    )(*args)


def _qk_rope_kernel(q_ref, k_ref, cos_ref, sin_ref, qg_ref, kg_ref, qo_ref, ko_ref, kb_ref, *, nsub, dqk, qscale):
    c = cos_ref[...]
    s = sin_ref[...]

    def norm_rope(x, g):
        y = x * lax.rsqrt(jnp.mean(x * x, axis=-1, keepdims=True) + RMS_EPS) * g
        return y * c + pltpu.roll(y, dqk // 2, axis=1) * s

    for idx in range(nsub):
        sl = slice(idx * dqk, (idx + 1) * dqk)
        qo_ref[0, :, sl] = (norm_rope(q_ref[0, :, sl], qg_ref[...]) * qscale).astype(qo_ref.dtype)
        kr = norm_rope(k_ref[0, :, sl], kg_ref[...])
        ko_ref[0, :, sl] = kr
        kb_ref[0, :, sl] = kr.astype(kb_ref.dtype)


def _qk_rope(proj3, cos, sin, qg, kg, *, qcols, dqk, qscale):
    b, s, _ = proj3.shape
    tm = _pick(s, 256, SUBLANES)
    blk = lambda c: pl.BlockSpec((1, tm, qcols), lambda bi, i, c=c: (bi, i, c))
    tab = pl.BlockSpec((tm, dqk), lambda bi, i: (i, 0))
    gain = pl.BlockSpec((1, dqk), lambda bi, i: (0, 0))
    out = pl.BlockSpec((1, tm, qcols), lambda bi, i: (bi, i, 0))
    return pl.pallas_call(
        functools.partial(_qk_rope_kernel, nsub=qcols // dqk, dqk=dqk, qscale=qscale),
        grid=(b, s // tm),
        in_specs=[blk(0), blk(1), tab, tab, gain, gain],
        out_specs=[out, out, out],
        out_shape=[jax.ShapeDtypeStruct((b, s, qcols), BF16),
                   jax.ShapeDtypeStruct((b, s, qcols), F32),
                   jax.ShapeDtypeStruct((b, s, qcols), BF16)],
        compiler_params=_params("parallel", "parallel"),
    )(proj3, proj3, cos, sin, qg.reshape(1, dqk), kg.reshape(1, dqk))


def _rope_tables(pos, dqk):
    half = dqk // 2
    inv = ROPE_THETA ** (-jnp.arange(half, dtype=F32) / half)
    ang = pos.astype(F32)[:, None] * inv[None, :]
    cos, sin = jnp.cos(ang), jnp.sin(ang)
    return jnp.concatenate([cos, cos], axis=-1), jnp.concatenate([-sin, sin], axis=-1)


def _lambda_value(lq1, lk1, lq2, lk2, lam_init):
    a = jnp.sum(lq1[...] * lk1[...], axis=-1, keepdims=True)
    b = jnp.sum(lq2[...] * lk2[...], axis=-1, keepdims=True)
    return jnp.exp(a) - jnp.exp(b) + lam_init


def _diff_combine(o0, o1, lam, subln, lam_init):
    da = o0 - lam * o1
    y = da * lax.rsqrt(jnp.mean(da * da, axis=-1, keepdims=True) + RMS_EPS) * subln
    return y * (1.0 - lam_init)


def _attn_prompt_kernel(qi_ref, kj_ref, q_ref, k_ref, v_ref, lq1, lk1, lq2, lk2, sub_ref, o_ref, m_ref, l_ref, acc_ref,
                        *, t, rc, dqk, lam_init):
    step = pl.program_id(2)
    i = qi_ref[step]
    j = kj_ref[step]

    @pl.when(j == 0)
    def _():
        m_ref[...] = jnp.full(m_ref.shape, NEG_INF, F32)
        l_ref[...] = jnp.zeros(l_ref.shape, F32)
        acc_ref[...] = jnp.zeros(acc_ref.shape, F32)

    def block(diagonal):
        vb = v_ref[0].astype(BF16)
        for c in range(t // rc):
            rows = slice(c * rc, (c + 1) * rc)
            kv = (c + 1) * rc if diagonal else t
            for mp in range(2):
                sl = slice(mp * dqk, (mp + 1) * dqk)
                s = lax.dot_general(q_ref[0, rows, sl], k_ref[0, 0:kv, sl], (((1,), (1,)), ((), ())),
                                    preferred_element_type=F32)
                if diagonal:
                    qpos = lax.broadcasted_iota(jnp.int32, (rc, kv), 0) + c * rc
                    kpos = lax.broadcasted_iota(jnp.int32, (rc, kv), 1)
                    s = jnp.where(kpos <= qpos, s, NEG_INF)
                m_prev = m_ref[mp, rows]
                m_new = jnp.maximum(m_prev, jnp.max(s, axis=-1, keepdims=True))
                corr = jnp.exp2(m_prev - m_new)
                p = jnp.exp2(s - m_new)
                l_ref[mp, rows] = l_ref[mp, rows] * corr + jnp.sum(p, axis=-1, keepdims=True)
                acc_ref[mp, rows] = acc_ref[mp, rows] * corr + jnp.dot(p.astype(BF16), vb[0:kv],
                                                                       preferred_element_type=F32)
                m_ref[mp, rows] = m_new

    @pl.when(j < i)
    def _():
        block(False)

    @pl.when(j == i)
    def _():
        block(True)
        lam = _lambda_value(lq1, lk1, lq2, lk2, lam_init)
        o0 = acc_ref[0] / l_ref[0]
        o1 = acc_ref[1] / l_ref[1]
        o_ref[0] = _diff_combine(o0, o1, lam, sub_ref[...], lam_init).astype(o_ref.dtype)


def _attn_prompt(qb, kb, proj3, lams, subln, *, heads, dqk, dv, vblock0, lam_init):
    b, s, _ = qb.shape
    t = _pick(s, 512, LANES)
    n = s // t
    pairs = [(i, j) for i in range(n) for j in range(i + 1)]
    qi = jnp.asarray([p[0] for p in pairs], jnp.int32)
    kj = jnp.asarray([p[1] for p in pairs], jnp.int32)
    qspec = pl.BlockSpec((1, t, 2 * dqk), lambda bi, h, st, qi, kj: (bi, qi[st], h))
    kspec = pl.BlockSpec((1, t, 2 * dqk), lambda bi, h, st, qi, kj: (bi, kj[st], h))
    vspec = pl.BlockSpec((1, t, dv), lambda bi, h, st, qi, kj: (bi, kj[st], vblock0 + h))
    vec = lambda w: pl.BlockSpec((1, w), lambda bi, h, st, qi, kj: (0, 0))
    grid_spec = pltpu.PrefetchScalarGridSpec(
        num_scalar_prefetch=2,
        grid=(b, heads, len(pairs)),
        in_specs=[qspec, kspec, vspec, vec(dqk), vec(dqk), vec(dqk), vec(dqk), vec(dv)],
        out_specs=pl.BlockSpec((1, t, dv), lambda bi, h, st, qi, kj: (bi, qi[st], h)),
        scratch_shapes=[pltpu.VMEM((2, t, 1), F32), pltpu.VMEM((2, t, 1), F32), pltpu.VMEM((2, t, dv), F32)],
    )
    return pl.pallas_call(
        functools.partial(_attn_prompt_kernel, t=t, rc=min(t, 2 * LANES), dqk=dqk, lam_init=lam_init),
        grid_spec=grid_spec,
        out_shape=jax.ShapeDtypeStruct((b, s, heads * dv), BF16),
        compiler_params=_params("parallel", "parallel", "arbitrary"),
    )(qi, kj, qb, kb, proj3, *[x.reshape(1, dqk) for x in lams], subln.reshape(1, dv))


PAGES_PER_STEP = 4


def _attn_paged_kernel(*refs, t_new, heads, npp, lam_init):
    pt_ref, q_ref = refs[0], refs[1]
    k_refs, v_refs = refs[2:2 + npp], refs[2 + npp:2 + 2 * npp]
    kn_ref, vn_ref, lq1, lk1, lq2, lk2, sub_ref, o_ref, m_ref, l_ref, acc_ref = refs[2 + 2 * npp:]
    del pt_ref
    p = pl.program_id(1)
    hq = heads * t_new
    kcols = v_refs[0].shape[0]

    @pl.when(p == 0)
    def _():
        m_ref[...] = jnp.full(m_ref.shape, NEG_INF, F32)
        l_ref[...] = jnp.zeros(l_ref.shape, F32)
        acc_ref[...] = jnp.zeros(acc_ref.shape, F32)

    def scores(kmats, ncols):
        parts = [lax.dot_general(q_ref[0, mp], kmats[mp].astype(BF16), (((1,), (1,)), ((), ())),
                                 preferred_element_type=F32) for mp in range(2)]
        s = jnp.concatenate(parts, axis=0)
        qrow = lax.broadcasted_iota(jnp.int32, (2 * hq, ncols), 0) % hq
        col = lax.broadcasted_iota(jnp.int32, (2 * hq, ncols), 1)
        return s, qrow, col

    def update(s_list, v_list):
        m_prev = m_ref[...]
        m_new = m_prev
        for s in s_list:
            m_new = jnp.maximum(m_new, jnp.max(s, axis=-1, keepdims=True))
        corr = jnp.exp(m_prev - m_new)
        l_new = l_ref[...] * corr
        acc = acc_ref[...] * corr
        for s, v in zip(s_list, v_list):
            pm = jnp.exp(s - m_new)
            l_new = l_new + jnp.sum(pm, axis=-1, keepdims=True)
            acc = acc + jnp.dot(pm.astype(BF16), v.astype(BF16), preferred_element_type=F32)
        l_ref[...] = l_new
        acc_ref[...] = acc
        m_ref[...] = m_new

    s_list = []
    for r in range(npp):
        kmats = [k_refs[r][pl.ds(mp, kcols, stride=2), :] for mp in range(2)]
        s, qrow, col = scores(kmats, kcols)
        s_list.append(jnp.where(col % heads == qrow // t_new, s, NEG_INF))
    update(s_list, [v_refs[r][...] for r in range(npp)])

    @pl.when(p == pl.num_programs(1) - 1)
    def _():
        ncols = vn_ref.shape[1]
        s, qrow, col = scores([kn_ref[0, 0], kn_ref[0, 1]], ncols)
        keep = (col % heads == qrow // t_new) & (col // heads <= qrow % t_new)
        update([jnp.where(keep, s, NEG_INF)], [vn_ref[0]])
        lam = _lambda_value(lq1, lk1, lq2, lk2, lam_init)
        o = acc_ref[...] / l_ref[...]
        o_ref[0] = _diff_combine(o[0:hq], o[hq:2 * hq], lam, sub_ref[...], lam_init)


def _attn_paged(page_table, qs, cache_k, cache_v, k_new, v_new, lams, subln, *, layer, t_new, lam_init):
    bd, n_pages = page_table.shape
    depth, n_pool, page, heads, _, dqk = cache_k.shape
    dv = cache_v.shape[-1]
    npp = PAGES_PER_STEP if n_pages % PAGES_PER_STEP == 0 else 1
    hq = heads * t_new
    ck = cache_k.reshape(depth, n_pool, page * heads * 2, dqk)
    cv = cache_v.reshape(depth, n_pool, page * heads, dv)
    ncols = v_new.shape[1]

    def cspec(r, rows, width):
        return pl.BlockSpec((None, None, rows, width), lambda b, p, pt, r=r: (layer, pt[b, p * npp + r], 0, 0))

    vec = lambda w: pl.BlockSpec((1, w), lambda b, p, pt: (0, 0))
    grid_spec = pltpu.PrefetchScalarGridSpec(
        num_scalar_prefetch=1,
        grid=(bd, n_pages // npp),
        in_specs=[pl.BlockSpec((1, 2, hq, dqk), lambda b, p, pt: (b, 0, 0, 0))]
        + [cspec(r, page * heads * 2, dqk) for r in range(npp)]
        + [cspec(r, page * heads, dv) for r in range(npp)]
        + [pl.BlockSpec((1, 2, ncols, dqk), lambda b, p, pt: (b, 0, 0, 0)),
           pl.BlockSpec((1, ncols, dv), lambda b, p, pt: (b, 0, 0)),
           vec(dqk), vec(dqk), vec(dqk), vec(dqk), vec(dv)],
        out_specs=pl.BlockSpec((1, hq, dv), lambda b, p, pt: (b, 0, 0)),
        scratch_shapes=[pltpu.VMEM((2 * hq, 1), F32), pltpu.VMEM((2 * hq, 1), F32), pltpu.VMEM((2 * hq, dv), F32)],
    )
    return pl.pallas_call(
        functools.partial(_attn_paged_kernel, t_new=t_new, heads=heads, npp=npp, lam_init=lam_init),
        grid_spec=grid_spec,
        out_shape=jax.ShapeDtypeStruct((bd, hq, dv), F32),
        compiler_params=_params("parallel", "arbitrary"),
    )(page_table, qs, *([ck] * npp), *([cv] * npp), k_new, v_new,
      *[x.reshape(1, dqk) for x in lams], subln.reshape(1, dv))


def _sconv_kernel(gb_ref, gc_ref, hc_ref, w_ref, hist_ref, o_ref, st_ref, ext_ref, *, tm):
    i = pl.program_id(1)

    @pl.when(i == 0)
    def _():
        ext_ref[SUBLANES - HIST:SUBLANES, :] = hist_ref[0]

    u = gc_ref[0] * hc_ref[0]
    ext_ref[SUBLANES:SUBLANES + tm, :] = u
    w = w_ref[...]
    cu = w[0:1] * ext_ref[SUBLANES - 2:SUBLANES - 2 + tm, :]
    cu = cu + w[1:2] * ext_ref[SUBLANES - 1:SUBLANES - 1 + tm, :]
    cu = cu + w[2:3] * u
    o_ref[0] = (gb_ref[0] * cu).astype(o_ref.dtype)
    last = ext_ref[tm + SUBLANES - HIST:tm + SUBLANES, :]
    st_ref[0] = last
    ext_ref[SUBLANES - HIST:SUBLANES, :] = last


def _sconv(proj3, w, hist, *, width, block0, out_dtype):
    b, s, _ = proj3.shape
    tm = _pick(s, 512, SUBLANES)
    blk = lambda c: pl.BlockSpec((1, tm, width), lambda bi, i, c=c: (bi, i, block0 + c))
    return pl.pallas_call(
        functools.partial(_sconv_kernel, tm=tm),
        grid=(b, s // tm),
        in_specs=[blk(0), blk(1), blk(2),
                  pl.BlockSpec((CONV_W, width), lambda bi, i: (0, 0)),
                  pl.BlockSpec((1, HIST, width), lambda bi, i: (bi, 0, 0))],
        out_specs=[pl.BlockSpec((1, tm, width), lambda bi, i: (bi, i, 0)),
                   pl.BlockSpec((1, HIST, width), lambda bi, i: (bi, 0, 0))],
        out_shape=[jax.ShapeDtypeStruct((b, s, width), out_dtype),
                   jax.ShapeDtypeStruct((b, HIST, width), F32)],
        scratch_shapes=[pltpu.VMEM((tm + SUBLANES, width), F32)],
        compiler_params=_params("parallel", "arbitrary"),
    )(proj3, proj3, proj3, w, hist)


def _mem_attn_kernel(q_ref, k_ref, v_ref, g_ref, o_ref, *, heads, hd):
    scale = hd ** -0.5
    for h in range(heads):
        sl = slice(h * hd, (h + 1) * hd)
        q = q_ref[0, :, sl]
        qn = q * lax.rsqrt(jnp.mean(q * q, axis=-1, keepdims=True) + RMS_EPS) * g_ref[...]
        s = lax.dot_general((qn * scale).astype(BF16), k_ref[0, :, sl].astype(BF16), (((1,), (1,)), ((), ())),
                            preferred_element_type=F32)
        p = jnp.exp(s - jnp.max(s, axis=-1, keepdims=True))
        o = jnp.dot(p.astype(BF16), v_ref[0, :, sl].astype(BF16), preferred_element_type=F32)
        o_ref[0, :, sl] = (o / jnp.sum(p, axis=-1, keepdims=True)).astype(o_ref.dtype)


def _mem_attn(q3, mk, mv, g, *, heads, hd, qblock, out_dtype):
    b, s, _ = q3.shape
    n = mk.shape[1]
    width = heads * hd
    tm = _pick(s, 512, SUBLANES)
    kv = pl.BlockSpec((1, n, width), lambda bi, i: (bi, 0, 0))
    return pl.pallas_call(
        functools.partial(_mem_attn_kernel, heads=heads, hd=hd),
        grid=(b, s // tm),
        in_specs=[pl.BlockSpec((1, tm, width), lambda bi, i: (bi, i, qblock)), kv, kv,
                  pl.BlockSpec((1, hd), lambda bi, i: (0, 0))],
        out_specs=pl.BlockSpec((1, tm, width), lambda bi, i: (bi, i, 0)),
        out_shape=jax.ShapeDtypeStruct((b, s, width), out_dtype),
        compiler_params=_params("parallel", "parallel"),
    )(q3, mk, mv, g.reshape(1, hd))


def _head_norm_kernel(x_ref, g_ref, o_ref, *, heads, hd):
    for h in range(heads):
        sl = slice(h * hd, (h + 1) * hd)
        x = x_ref[:, sl]
        o_ref[:, sl] = x * lax.rsqrt(jnp.mean(x * x, axis=-1, keepdims=True) + RMS_EPS) * g_ref[...]


def _head_norm(x, g, *, heads, hd):
    m, width = x.shape
    tm = _pick(m, 512, SUBLANES)
    return pl.pallas_call(
        functools.partial(_head_norm_kernel, heads=heads, hd=hd),
        grid=(m // tm,),
        in_specs=[pl.BlockSpec((tm, width), lambda i: (i, 0)), pl.BlockSpec((1, hd), lambda i: (0, 0))],
        out_specs=pl.BlockSpec((tm, width), lambda i: (i, 0)),
        out_shape=jax.ShapeDtypeStruct((m, width), F32),
        compiler_params=_params("parallel"),
    )(x, g.reshape(1, hd))


def _ffn_activation(g, u, ext_ref, w_ref, b_ref, tm):
    ext_ref[SUBLANES:SUBLANES + tm, :] = g
    w = w_ref[...]
    gc = w[0:1] * ext_ref[SUBLANES - 2:SUBLANES - 2 + tm, :]
    gc = gc + w[1:2] * ext_ref[SUBLANES - 1:SUBLANES - 1 + tm, :]
    gc = gc + w[2:3] * g
    gc = gc + b_ref[...]
    return gc * (1.0 / (1.0 + jnp.exp(-gc))) * u


def _ffn_fused_kernel(x_ref, wg_ref, wu_ref, cw_ref, cb_ref, hist_ref, a_ref, st_ref, ext_ref, *, s, rc):
    ext_ref[SUBLANES - HIST:SUBLANES, :] = hist_ref[0]
    w = cw_ref[...]
    bias = cb_ref[...]
    for c in range(s // rc):
        r0 = c * rc
        x = x_ref[0, r0:r0 + rc, :]
        g = jnp.dot(x, wg_ref[...], preferred_element_type=F32)
        u = jnp.dot(x, wu_ref[...], preferred_element_type=F32)
        ext_ref[SUBLANES + r0:SUBLANES + r0 + rc, :] = g
        gc = w[0:1] * ext_ref[SUBLANES - 2 + r0:SUBLANES - 2 + r0 + rc, :]
        gc = gc + w[1:2] * ext_ref[SUBLANES - 1 + r0:SUBLANES - 1 + r0 + rc, :]
        gc = gc + w[2:3] * g
        gc = gc + bias
        a_ref[0, r0:r0 + rc, :] = (gc * (1.0 / (1.0 + jnp.exp(-gc))) * u).astype(a_ref.dtype)
    st_ref[0] = ext_ref[s + SUBLANES - HIST:s + SUBLANES, :]


def _ffn_fused(h3, wg, wu, cw, cb, hist, *, tn=256, rc=512):
    b, s, d = h3.shape
    f = wg.shape[1]
    tn = _pick(f, tn, LANES)
    rc = _pick(s, rc, SUBLANES)
    wspec = pl.BlockSpec((d, tn), lambda bi, j: (0, j))
    hspec = pl.BlockSpec((1, HIST, tn), lambda bi, j: (bi, 0, j))
    return pl.pallas_call(
        functools.partial(_ffn_fused_kernel, s=s, rc=rc),
        grid=(b, f // tn),
        in_specs=[pl.BlockSpec((1, s, d), lambda bi, j: (bi, 0, 0)), wspec, wspec,
                  pl.BlockSpec((CONV_W, tn), lambda bi, j: (0, j)),
                  pl.BlockSpec((1, tn), lambda bi, j: (0, j)), hspec],
        out_specs=[pl.BlockSpec((1, s, tn), lambda bi, j: (bi, 0, j)), hspec],
        out_shape=[jax.ShapeDtypeStruct((b, s, f), BF16), jax.ShapeDtypeStruct((b, HIST, f), F32)],
        scratch_shapes=[pltpu.VMEM((s + SUBLANES, tn), F32)],
        compiler_params=_params("parallel", "parallel"),
    )(h3, wg, wu, cw, cb.reshape(1, f), hist)


def _ffn_act_kernel(g_ref, u_ref, cw_ref, cb_ref, hist_ref, a_ref, st_ref, ext_ref, *, tm):
    ext_ref[SUBLANES - HIST:SUBLANES, :] = hist_ref[0]
    a_ref[0] = _ffn_activation(g_ref[0], u_ref[0], ext_ref, cw_ref, cb_ref, tm).astype(a_ref.dtype)
    st_ref[0] = ext_ref[tm + SUBLANES - HIST:tm + SUBLANES, :]


def _ffn_act(g3, u3, cw, cb, hist, *, tn=1024):
    b, t, f = g3.shape
    tn = _pick(f, tn, LANES)
    blk = pl.BlockSpec((1, t, tn), lambda bi, j: (bi, 0, j))
    hspec = pl.BlockSpec((1, HIST, tn), lambda bi, j: (bi, 0, j))
    return pl.pallas_call(
        functools.partial(_ffn_act_kernel, tm=t),
        grid=(b, f // tn),
        in_specs=[blk, blk, pl.BlockSpec((CONV_W, tn), lambda bi, j: (0, j)),
                  pl.BlockSpec((1, tn), lambda bi, j: (0, j)), hspec],
        out_specs=[blk, hspec],
        out_shape=[jax.ShapeDtypeStruct((b, t, f), F32), jax.ShapeDtypeStruct((b, HIST, f), F32)],
        scratch_shapes=[pltpu.VMEM((t + SUBLANES, tn), F32)],
        compiler_params=_params("parallel", "parallel"),
    )(g3, u3, cw, cb.reshape(1, f), hist)


def kernel(x_prompt, x_sample, cache_k, cache_v, cache_mem_k, cache_mem_v, state_conv, state_ffn, page_table, mem_prompt, norm_mix, w_in, q_norm, k_norm, lambda_q1, lambda_k1, lambda_q2, lambda_k2, subln, sc_conv_w, mq_norm, norm_mem, w_mem_k, w_mem_v, mk_norm, w_out, norm_ffn, w_gate, w_up, ffn_conv_w, ffn_conv_b, w_down):
    b, s, d = x_prompt.shape
    bd, t, _ = x_sample.shape
    depth, n_pool, page, heads, _, dqk = cache_k.shape
    dv = cache_v.shape[-1]
    qcols = heads * 2 * dqk
    da_width = heads * dv
    sc_width = sc_conv_w.shape[-1]
    n_mem, mem_heads, mem_hd = cache_mem_k.shape[2:]
    mem_width = mem_heads * mem_hd
    d_ff = w_gate.shape[-1]
    n_pages = page_table.shape[1]
    past_len = n_pages * page
    c2 = 2 * qcols + da_width
    c5 = c2 + 3 * sc_width
    assert qcols == da_width and dv == 2 * dqk
    assert c2 % sc_width == 0 and c5 % mem_width == 0 and (2 * qcols) % dv == 0
    assert t <= page and (2 * heads * t) % SUBLANES == 0
    sc_block0, mq_block, v_block0 = c2 // sc_width, c5 // mem_width, (2 * qcols) // dv

    cos_p, sin_p = _rope_tables(jnp.arange(s, dtype=jnp.int32), dqk)
    cos_s, sin_s = _rope_tables(past_len + jnp.arange(t, dtype=jnp.int32), dqk)
    t_mq = 16
    assert t <= t_mq
    new_cols = -(-t * heads // LANES) * LANES

    xp = x_prompt.reshape(b * s, d)
    xs = x_sample.reshape(bd * t, d)
    mem2 = mem_prompt.reshape(b * n_mem, d)
    zeros_conv = jnp.zeros((b, HIST, sc_width), F32)
    zeros_ffn = jnp.zeros((b, HIST, d_ff), F32)

    outs = {k: [] for k in ("kp", "vp", "cp", "fp", "mkp", "mvp", "ks", "vs", "cs", "fs")}
    for l in range(depth):
        lam_init = _lambda_init(l)
        lams = (lambda_q1[l], lambda_k1[l], lambda_q2[l], lambda_k2[l])
        w_in_b = w_in[l].astype(BF16)
        w_out_b = w_out[l].astype(BF16)
        w_gate_b = w_gate[l].astype(BF16)
        w_up_b = w_up[l].astype(BF16)
        w_down_b = w_down[l].astype(BF16)

        hm = _rmsnorm(mem2, norm_mem[l])
        mk = _head_norm(_matmul(hm, w_mem_k[l].astype(BF16)), mk_norm[l], heads=mem_heads, hd=mem_hd)
        mv = _matmul(hm, w_mem_v[l].astype(BF16))
        mk3 = mk.reshape(b, n_mem, mem_width)
        mv3 = mv.reshape(b, n_mem, mem_width)

        proj3 = _matmul(_rmsnorm(xp, norm_mix[l]), w_in_b).reshape(b, s, -1)
        qb, kf, kb = _qk_rope(proj3, cos_p, sin_p, q_norm[l], k_norm[l], qcols=qcols, dqk=dqk,
                              qscale=dqk ** -0.5 * math.log2(math.e))
        da = _attn_prompt(qb, kb, proj3, lams, subln[l], heads=heads, dqk=dqk, dv=dv, vblock0=v_block0,
                          lam_init=lam_init)
        sc, conv_state = _sconv(proj3, sc_conv_w[l], zeros_conv, width=sc_width, block0=sc_block0, out_dtype=BF16)
        mo = _mem_attn(proj3, mk3, mv3, mq_norm[l], heads=mem_heads, hd=mem_hd, qblock=mq_block, out_dtype=BF16)
        xp = _matmul_parts([da.reshape(b * s, -1), sc.reshape(b * s, -1), mo.reshape(b * s, -1)], w_out_b, xp)
        h2 = _rmsnorm(xp, norm_ffn[l]).reshape(b, s, d)
        act, ffn_state = _ffn_fused(h2, w_gate_b, w_up_b, ffn_conv_w[l], ffn_conv_b[l], zeros_ffn)
        xp = _matmul(act.reshape(b * s, d_ff), w_down_b, xp, tn=512, tk=d_ff // 2)

        outs["kp"].append(kf.reshape(b, s, heads, 2, dqk))
        outs["vp"].append(proj3[:, :, 2 * qcols:c2].reshape(b, s, heads, dv))
        outs["cp"].append(conv_state)
        outs["fp"].append(ffn_state)
        outs["mkp"].append(mk.reshape(b, n_mem, mem_heads, mem_hd))
        outs["mvp"].append(mv.reshape(b, n_mem, mem_heads, mem_hd))

        proj3 = _matmul(_rmsnorm(xs, norm_mix[l]), w_in_b).reshape(bd, t, -1)
        qb, kf, _ = _qk_rope(proj3, cos_s, sin_s, q_norm[l], k_norm[l], qcols=qcols, dqk=dqk, qscale=dqk ** -0.5)
        v_new = proj3[:, :, 2 * qcols:c2]
        qs = qb.reshape(bd, t, heads, 2, dqk).transpose(0, 3, 2, 1, 4).reshape(bd, 2, heads * t, dqk)
        kn = kf.reshape(bd, t, heads, 2, dqk).transpose(0, 3, 1, 2, 4).reshape(bd, 2, t * heads, dqk)
        kn = jnp.pad(kn, ((0, 0), (0, 0), (0, new_cols - t * heads), (0, 0)))
        vn = jnp.pad(v_new.reshape(bd, t * heads, dv), ((0, 0), (0, new_cols - t * heads), (0, 0)))
        da = _attn_paged(page_table, qs, cache_k, cache_v, kn, vn, lams, subln[l], layer=l, t_new=t,
                         lam_init=lam_init)
        da = da.reshape(bd, heads, t, dv).transpose(0, 2, 1, 3).reshape(bd, t, da_width)
        sc, conv_state = _sconv(proj3, sc_conv_w[l], state_conv[l], width=sc_width, block0=sc_block0, out_dtype=F32)
        mq = jnp.pad(proj3[:, :, c5:], ((0, 0), (0, t_mq - t), (0, 0)))
        mo = _mem_attn(mq, cache_mem_k[l].reshape(bd, n_mem, mem_width), cache_mem_v[l].reshape(bd, n_mem, mem_width),
                       mq_norm[l], heads=mem_heads, hd=mem_hd, qblock=0, out_dtype=F32)[:, :t]
        mix = jnp.concatenate([da, sc, mo], axis=-1).astype(BF16).reshape(bd * t, -1)
        xs = _matmul(mix, w_out_b, xs)
        h2 = _rmsnorm(xs, norm_ffn[l])
        g3 = _matmul(h2, w_gate_b, tn=256).reshape(bd, t, d_ff)
        u3 = _matmul(h2, w_up_b, tn=256).reshape(bd, t, d_ff)
        act, ffn_state = _ffn_act(g3, u3, ffn_conv_w[l], ffn_conv_b[l], state_ffn[l], tn=d_ff // 2)
        xs = _matmul(act.astype(BF16).reshape(bd * t, d_ff), w_down_b, xs, tn=512, tk=d_ff // 2)

        outs["ks"].append(kf.reshape(bd, t, heads, 2, dqk))
        outs["vs"].append(v_new.reshape(bd, t, heads, dv))
        outs["cs"].append(conv_state)
        outs["fs"].append(ffn_state)

    st = lambda k: jnp.stack(outs[k])
    return (xp.reshape(b, s, d), xs.reshape(bd, t, d),
            st("kp"), st("vp"), st("cp"), st("fp"), st("mkp"), st("mvp"),
            st("ks"), st("vs"), st("cs"), st("fs"))
```

```python
import functools
import math

import jax
import jax.numpy as jnp
from jax import lax
from jax.experimental import pallas as pl
from jax.experimental.pallas import tpu as pltpu

ROPE_THETA = 10000.0
RMS_EPS = 1e-6
NEG_INF = -1e30
CONV_W = 3
HIST = CONV_W - 1
V7X_VMEM_BYTES = 64 * 1024 * 1024
VMEM_LIMIT = V7X_VMEM_BYTES - 8 * 1024 * 1024
SUBLANES = 8
LANES = 128

F32 = jnp.float32
BF16 = jnp.bfloat16


def _params(*sem):
    return pltpu.CompilerParams(dimension_semantics=sem, vmem_limit_bytes=VMEM_LIMIT)


def _pick(n, pref, align):
    if n <= pref:
        return n
    t = (pref // align) * align
    while t >= align:
        if n % t == 0:
            return t
        t -= align
    return n


def _lambda_init(l):
    return 0.8 - 0.6 * math.exp(-0.3 * l)


def _rmsnorm_kernel(x_ref, g_ref, o_ref):
    x = x_ref[...]
    ms = jnp.mean(x * x, axis=-1, keepdims=True)
    o_ref[...] = (x * lax.rsqrt(ms + RMS_EPS) * g_ref[...]).astype(o_ref.dtype)


def _rmsnorm(x, g):
    m, d = x.shape
    tm = _pick(m, 256, SUBLANES)
    return pl.pallas_call(
        _rmsnorm_kernel,
        grid=(m // tm,),
        in_specs=[pl.BlockSpec((tm, d), lambda i: (i, 0)),
                  pl.BlockSpec((1, d), lambda i: (0, 0))],
        out_specs=pl.BlockSpec((tm, d), lambda i: (i, 0)),
        out_shape=jax.ShapeDtypeStruct((m, d), BF16),
        compiler_params=_params("parallel"),
    )(x, g.reshape(1, d))


def _matmul_kernel(*refs, nparts, nk, has_res):
    xs, ws = refs[:nparts], refs[nparts:2 * nparts]
    r_ref = refs[2 * nparts] if has_res else None
    o_ref = refs[-1]
    d = None
    for x_ref, w_ref in zip(xs, ws):
        part = jnp.dot(x_ref[...], w_ref[...], preferred_element_type=F32)
        d = part if d is None else d + part
    if nk == 1:
        o_ref[...] = r_ref[...] + d if has_res else d
    else:
        k = pl.program_id(2)

        @pl.when(k == 0)
        def _():
            o_ref[...] = r_ref[...] + d if has_res else d

        @pl.when(k > 0)
        def _():
            o_ref[...] += d


def _matmul(xs, w, layer, res=None, *, tm=1024, tn=1024, tk=None):
    xs = list(xs) if isinstance(xs, (list, tuple)) else [xs]
    m = xs[0].shape[0]
    kdim, n = w.shape[1], w.shape[2]
    tm = _pick(m, tm, SUBLANES)
    tn = _pick(n, tn, LANES)
    nk = 1 if tk is None else kdim // tk
    assert nk == 1 or len(xs) == 1
    in_specs, w_specs, off = [], [], 0
    for x in xs:
        kp = x.shape[1] if nk == 1 else tk
        assert off % kp == 0
        in_specs.append(pl.BlockSpec((tm, kp), lambda i, j, k: (i, k)))
        w_specs.append(pl.BlockSpec((None, kp, tn), lambda i, j, k, r=off // kp: (layer, r + k, j)))
        off += x.shape[1]
    assert off == kdim
    in_specs += w_specs
    args = xs + [w] * len(xs)
    if res is not None:
        in_specs.append(pl.BlockSpec((tm, tn), lambda i, j, k: (i, j)))
        args.append(res)
    return pl.pallas_call(
        functools.partial(_matmul_kernel, nparts=len(xs), nk=nk, has_res=res is not None),
        grid=(m // tm, n // tn, nk),
        in_specs=in_specs,
        out_specs=pl.BlockSpec((tm, tn), lambda i, j, k: (i, j)),
        out_shape=jax.ShapeDtypeStruct((m, n), F32),
        compiler_params=_params("parallel", "parallel", "arbitrary"),
    )(*args)


def _matmul_cast_kernel(*refs, has_res):
    if has_res:
        x_ref, w_ref, r_ref, o_ref, wb_ref = refs
    else:
        x_ref, w_ref, o_ref, wb_ref = refs
    wb = w_ref[...].astype(BF16)
    wb_ref[...] = wb
    d = jnp.dot(x_ref[...], wb, preferred_element_type=F32)
    o_ref[...] = r_ref[...] + d if has_res else d


def _matmul_cast(x, w, layer, res=None):
    m, kdim = x.shape
    n = w.shape[2]
    tn = _pick(n, max(LANES, (8 * 1024 * 1024) // (4 * kdim) // LANES * LANES), LANES)
    in_specs = [pl.BlockSpec((m, kdim), lambda j: (0, 0)),
                pl.BlockSpec((None, kdim, tn), lambda j: (layer, 0, j))]
    args = [x, w]
    if res is not None:
        in_specs.append(pl.BlockSpec((m, tn), lambda j: (0, j)))
        args.append(res)
    return pl.pallas_call(
        functools.partial(_matmul_cast_kernel, has_res=res is not None),
        grid=(n // tn,),
        in_specs=in_specs,
        out_specs=[pl.BlockSpec((m, tn), lambda j: (0, j)), pl.BlockSpec((None, kdim, tn), lambda j: (0, 0, j))],
        out_shape=[jax.ShapeDtypeStruct((m, n), F32), jax.ShapeDtypeStruct((1, kdim, n), BF16)],
        compiler_params=_params("parallel"),
    )(*args)


def _qk_rope_kernel(q_ref, k_ref, cos_ref, sin_ref, qg_ref, kg_ref, qo_ref, ko_ref, kb_ref, *, nsub, dqk, qscale):
    c = cos_ref[...]
    s = sin_ref[...]

    def norm_rope(x, g):
        y = x * lax.rsqrt(jnp.mean(x * x, axis=-1, keepdims=True) + RMS_EPS) * g
        return y * c + pltpu.roll(y, dqk // 2, axis=1) * s

    for idx in range(nsub):
        sl = slice(idx * dqk, (idx + 1) * dqk)
        qo_ref[0, :, sl] = (norm_rope(q_ref[0, :, sl], qg_ref[...]) * qscale).astype(qo_ref.dtype)
        kr = norm_rope(k_ref[0, :, sl], kg_ref[...])
        ko_ref[0, :, sl] = kr
        kb_ref[0, :, sl] = kr.astype(kb_ref.dtype)


def _qk_rope(proj3, cos, sin, qg, kg, *, qcols, dqk, qscale):
    b, s, _ = proj3.shape
    tm = _pick(s, 256, SUBLANES)
    blk = lambda c: pl.BlockSpec((1, tm, qcols), lambda bi, i, c=c: (bi, i, c))
    tab = pl.BlockSpec((tm, dqk), lambda bi, i: (i, 0))
    gain = pl.BlockSpec((1, dqk), lambda bi, i: (0, 0))
    out = pl.BlockSpec((1, tm, qcols), lambda bi, i: (bi, i, 0))
    return pl.pallas_call(
        functools.partial(_qk_rope_kernel, nsub=qcols // dqk, dqk=dqk, qscale=qscale),
        grid=(b, s // tm),
        in_specs=[blk(0), blk(1), tab, tab, gain, gain],
        out_specs=[out, out, out],
        out_shape=[jax.ShapeDtypeStruct((b, s, qcols), BF16),
                   jax.ShapeDtypeStruct((b, s, qcols), F32),
                   jax.ShapeDtypeStruct((b, s, qcols), BF16)],
        compiler_params=_params("parallel", "parallel"),
    )(proj3, proj3, cos, sin, qg.reshape(1, dqk), kg.reshape(1, dqk))


def _rope_tables(pos, dqk):
    half = dqk // 2
    inv = ROPE_THETA ** (-jnp.arange(half, dtype=F32) / half)
    ang = pos.astype(F32)[:, None] * inv[None, :]
    cos, sin = jnp.cos(ang), jnp.sin(ang)
    return jnp.concatenate([cos, cos], axis=-1), jnp.concatenate([-sin, sin], axis=-1)


def _lambda_value(lq1, lk1, lq2, lk2, lam_init):
    a = jnp.sum(lq1[...] * lk1[...], axis=-1, keepdims=True)
    b = jnp.sum(lq2[...] * lk2[...], axis=-1, keepdims=True)
    return jnp.exp(a) - jnp.exp(b) + lam_init


def _diff_combine(o0, o1, lam, subln, lam_init):
    da = o0 - lam * o1
    y = da * lax.rsqrt(jnp.mean(da * da, axis=-1, keepdims=True) + RMS_EPS) * subln
    return y * (1.0 - lam_init)


def _attn_prompt_kernel(qi_ref, kj_ref, q_ref, k_ref, v_ref, lq1, lk1, lq2, lk2, sub_ref, o_ref, m_ref, l_ref, acc_ref,
                        *, t, rc, dqk, dv, lam_init):
    step = pl.program_id(2)
    i = qi_ref[step]
    j = kj_ref[step]

    @pl.when(j == 0)
    def _():
        m_ref[...] = jnp.full(m_ref.shape, NEG_INF, F32)
        l_ref[...] = jnp.zeros(l_ref.shape, F32)
        acc_ref[...] = jnp.zeros(acc_ref.shape, F32)

    def block(diagonal):
        vb = v_ref[0].astype(BF16)
        for c in range(t // rc):
            rows = slice(c * rc, (c + 1) * rc)
            kv = (c + 1) * rc if diagonal else t
            for mp in range(2):
                sl = slice(mp * dqk, (mp + 1) * dqk)
                s = lax.dot_general(q_ref[0, rows, sl], k_ref[0, 0:kv, sl], (((1,), (1,)), ((), ())),
                                    preferred_element_type=F32)
                if diagonal:
                    qpos = lax.broadcasted_iota(jnp.int32, (rc, kv), 0) + c * rc
                    kpos = lax.broadcasted_iota(jnp.int32, (rc, kv), 1)
                    s = jnp.where(kpos <= qpos, s, NEG_INF)
                m_prev = m_ref[mp, rows]
                m_new = jnp.maximum(m_prev, jnp.max(s, axis=-1, keepdims=True))
                corr = jnp.exp2(m_prev - m_new)
                p = jnp.exp2(s - pltpu.repeat(m_new, kv // LANES, axis=1))
                l_ref[mp, rows] = l_ref[mp, rows] * corr + jnp.sum(p, axis=-1, keepdims=True)
                acc_ref[mp, rows] = (acc_ref[mp, rows] * pltpu.repeat(corr, dv // LANES, axis=1)
                                     + jnp.dot(p.astype(BF16), vb[0:kv], preferred_element_type=F32))
                m_ref[mp, rows] = m_new

    @pl.when(j < i)
    def _():
        block(False)

    @pl.when(j == i)
    def _():
        block(True)
        lam = _lambda_value(lq1, lk1, lq2, lk2, lam_init)
        o0 = acc_ref[0] / pltpu.repeat(l_ref[0], dv // LANES, axis=1)
        o1 = acc_ref[1] / pltpu.repeat(l_ref[1], dv // LANES, axis=1)
        o_ref[0] = _diff_combine(o0, o1, lam, sub_ref[...], lam_init).astype(o_ref.dtype)


def _attn_prompt(qb, kb, proj3, lams, subln, *, heads, dqk, dv, vblock0, lam_init):
    b, s, _ = qb.shape
    t = _pick(s, 1024, LANES)
    n = s // t
    pairs =[(i, j) for i in range(n) for j in range(i + 1)]
    qi = jnp.asarray([p[0] for p in pairs], jnp.int32)
    kj = jnp.asarray([p[1] for p in pairs], jnp.int32)
    qspec = pl.BlockSpec((1, t, 2 * dqk), lambda bi, h, st, qi, kj: (bi, qi[st], h))
    kspec = pl.BlockSpec((1, t, 2 * dqk), lambda bi, h, st, qi, kj: (bi, kj[st], h))
    vspec = pl.BlockSpec((1, t, dv), lambda bi, h, st, qi, kj: (bi, kj[st], vblock0 + h))
    vec = lambda w: pl.BlockSpec((1, w), lambda bi, h, st, qi, kj: (0, 0))
    grid_spec = pltpu.PrefetchScalarGridSpec(
        num_scalar_prefetch=2,
        grid=(b, heads, len(pairs)),
        in_specs=[qspec, kspec, vspec, vec(dqk), vec(dqk), vec(dqk), vec(dqk), vec(dv)],
        out_specs=pl.BlockSpec((1, t, dv), lambda bi, h, st, qi, kj: (bi, qi[st], h)),
        scratch_shapes=[pltpu.VMEM((2, t, LANES), F32), pltpu.VMEM((2, t, LANES), F32), pltpu.VMEM((2, t, dv), F32)],
    )
    return pl.pallas_call(
        functools.partial(_attn_prompt_kernel, t=t, rc=min(t, 2 * LANES), dqk=dqk, dv=dv, lam_init=lam_init),
        grid_spec=grid_spec,
        out_shape=jax.ShapeDtypeStruct((b, s, heads * dv), BF16),
        compiler_params=_params("parallel", "parallel", "arbitrary"),
    )(qi, kj, qb, kb, proj3, *[x.reshape(1, dqk) for x in lams], subln.reshape(1, dv))


PAGES_PER_STEP = 8


def _attn_paged_kernel(*refs, t_new, heads, npp, lam_init):
    pt_ref, q_ref = refs[0], refs[1]
    k_refs, v_refs = refs[2:2 + npp], refs[2 + npp:2 + 2 * npp]
    kn_ref, vn_ref, lq1, lk1, lq2, lk2, sub_ref, o_ref, m_ref, l_ref, acc_ref = refs[2 + 2 * npp:]
    del pt_ref
    p = pl.program_id(1)
    hq = heads * t_new
    kcols = v_refs[0].shape[0]

    @pl.when(p == 0)
    def _():
        m_ref[...] = jnp.full(m_ref.shape, NEG_INF, F32)
        l_ref[...] = jnp.zeros(l_ref.shape, F32)
        acc_ref[...] = jnp.zeros(acc_ref.shape, F32)

    def scores(kmats, ncols):
        parts = [lax.dot_general(q_ref[0, mp], kmats[mp].astype(BF16), (((1,), (1,)), ((), ())),
                                 preferred_element_type=F32) for mp in range(2)]
        s = jnp.concatenate(parts, axis=0)
        qrow = lax.broadcasted_iota(jnp.int32, (2 * hq, ncols), 0) % hq
        col = lax.broadcasted_iota(jnp.int32, (2 * hq, ncols), 1)
        return s, qrow, col

    def update(s_list, v_list):
        m_prev = m_ref[...]
        m_new = m_prev
        for s in s_list:
            m_new = jnp.maximum(m_new, jnp.max(s, axis=-1, keepdims=True))
        corr = jnp.exp(m_prev - m_new)
        l_new = l_ref[...] * corr
        acc = acc_ref[...] * corr
        for s, v in zip(s_list, v_list):
            pm = jnp.exp(s - m_new)
            l_new = l_new + jnp.sum(pm, axis=-1, keepdims=True)
            acc = acc + jnp.dot(pm.astype(BF16), v.astype(BF16), preferred_element_type=F32)
        l_ref[...] = l_new
        acc_ref[...] = acc
        m_ref[...] = m_new

    s_list = []
    for r in range(npp):
        kmats = [k_refs[r][pl.ds(mp, kcols, stride=2), :] for mp in range(2)]
        s, qrow, col = scores(kmats, kcols)
        s_list.append(jnp.where(col % heads == qrow // t_new, s, NEG_INF))
    update(s_list, [v_refs[r][...] for r in range(npp)])

    @pl.when(p == pl.num_programs(1) - 1)
    def _():
        ncols = vn_ref.shape[1]
        s, qrow, col = scores([kn_ref[0, 0], kn_ref[0, 1]], ncols)
        keep = (col % heads == qrow // t_new) & (col // heads <= qrow % t_new)
        update([jnp.where(keep, s, NEG_INF)], [vn_ref[0]])
        lam = _lambda_value(lq1, lk1, lq2, lk2, lam_init)
        o = acc_ref[...] / l_ref[...]
        o_ref[0] = _diff_combine(o[0:hq], o[hq:2 * hq], lam, sub_ref[...], lam_init)


def _attn_paged(page_table, qs, cache_k, cache_v, k_new, v_new, lams, subln, *, layer, t_new, lam_init):
    bd, n_pages = page_table.shape
    depth, n_pool, page, heads, _, dqk = cache_k.shape
    dv = cache_v.shape[-1]
    npp = PAGES_PER_STEP if n_pages % PAGES_PER_STEP == 0 else 1
    hq = heads * t_new
    ck = cache_k.reshape(depth, n_pool, page * heads * 2, dqk)
    cv = cache_v.reshape(depth, n_pool, page * heads, dv)
    ncols = v_new.shape[1]

    def cspec(r, rows, width):
        return pl.BlockSpec((None, None, rows, width), lambda b, p, pt, r=r: (layer, pt[b, p * npp + r], 0, 0))

    vec = lambda w: pl.BlockSpec((1, w), lambda b, p, pt: (0, 0))
    grid_spec = pltpu.PrefetchScalarGridSpec(
        num_scalar_prefetch=1,
        grid=(bd, n_pages // npp),
        in_specs=[pl.BlockSpec((1, 2, hq, dqk), lambda b, p, pt: (b, 0, 0, 0))]
        + [cspec(r, page * heads * 2, dqk) for r in range(npp)]
        + [cspec(r, page * heads, dv) for r in range(npp)]
        + [pl.BlockSpec((1, 2, ncols, dqk), lambda b, p, pt: (b, 0, 0, 0)),
           pl.BlockSpec((1, ncols, dv), lambda b, p, pt: (b, 0, 0)),
           vec(dqk), vec(dqk), vec(dqk), vec(dqk), vec(dv)],
        out_specs=pl.BlockSpec((1, hq, dv), lambda b, p, pt: (b, 0, 0)),
        scratch_shapes=[pltpu.VMEM((2 * hq, 1), F32), pltpu.VMEM((2 * hq, 1), F32), pltpu.VMEM((2 * hq, dv), F32)],
    )
    return pl.pallas_call(
        functools.partial(_attn_paged_kernel, t_new=t_new, heads=heads, npp=npp, lam_init=lam_init),
        grid_spec=grid_spec,
        out_shape=jax.ShapeDtypeStruct((bd, hq, dv), F32),
        compiler_params=_params("parallel", "arbitrary"),
    )(page_table, qs, *([ck] * npp), *([cv] * npp), k_new, v_new,
      *[x.reshape(1, dqk) for x in lams], subln.reshape(1, dv))


def _sconv_kernel(gb_ref, gc_ref, hc_ref, w_ref, hist_ref, o_ref, st_ref, ext_ref, *, tm):
    i = pl.program_id(1)

    @pl.when(i == 0)
    def _():
        ext_ref[SUBLANES - HIST:SUBLANES, :] = hist_ref[0]

    u = gc_ref[0] * hc_ref[0]
    ext_ref[SUBLANES:SUBLANES + tm, :] = u
    w = w_ref[...]
    cu = w[0:1] * ext_ref[SUBLANES - 2:SUBLANES - 2 + tm, :]
    cu = cu + w[1:2] * ext_ref[SUBLANES - 1:SUBLANES - 1 + tm, :]
    cu = cu + w[2:3] * u
    o_ref[0] = (gb_ref[0] * cu).astype(o_ref.dtype)
    last = ext_ref[tm + SUBLANES - HIST:tm + SUBLANES, :]
    st_ref[0] = last
    ext_ref[SUBLANES - HIST:SUBLANES, :] = last


def _sconv(proj3, w, hist, *, width, block0, out_dtype):
    b, s, _ = proj3.shape
    tm = _pick(s, 512, SUBLANES)
    blk = lambda c: pl.BlockSpec((1, tm, width), lambda bi, i, c=c: (bi, i, block0 + c))
    return pl.pallas_call(
        functools.partial(_sconv_kernel, tm=tm),
        grid=(b, s // tm),
        in_specs=[blk(0), blk(1), blk(2),
                  pl.BlockSpec((CONV_W, width), lambda bi, i: (0, 0)),
                  pl.BlockSpec((1, HIST, width), lambda bi, i: (bi, 0, 0))],
        out_specs=[pl.BlockSpec((1, tm, width), lambda bi, i: (bi, i, 0)),
                   pl.BlockSpec((1, HIST, width), lambda bi, i: (bi, 0, 0))],
        out_shape=[jax.ShapeDtypeStruct((b, s, width), out_dtype),
                   jax.ShapeDtypeStruct((b, HIST, width), F32)],
        scratch_shapes=[pltpu.VMEM((tm + SUBLANES, width), F32)],
        compiler_params=_params("parallel", "arbitrary"),
    )(proj3, proj3, proj3, w, hist)


def _mem_attn_kernel(q_ref, k_ref, v_ref, g_ref, o_ref, *, heads, hd):
    scale = hd ** -0.5
    for h in range(heads):
        sl = slice(h * hd, (h + 1) * hd)
        q = q_ref[0, :, sl]
        qn = q * lax.rsqrt(jnp.mean(q * q, axis=-1, keepdims=True) + RMS_EPS) * g_ref[...]
        s = lax.dot_general((qn * scale).astype(BF16), k_ref[0, :, sl].astype(BF16), (((1,), (1,)), ((), ())),
                            preferred_element_type=F32)
        p = jnp.exp(s - jnp.max(s, axis=-1, keepdims=True))
        o = jnp.dot(p.astype(BF16), v_ref[0, :, sl].astype(BF16), preferred_element_type=F32)
        o_ref[0, :, sl] = (o / jnp.sum(p, axis=-1, keepdims=True)).astype(o_ref.dtype)


def _mem_attn(q3, mk, mv, g, *, heads, hd, qblock, out_dtype):
    b, s, _ = q3.shape
    n = mk.shape[1]
    width = heads * hd
    tm = _pick(s, 512, SUBLANES)
    kv = pl.BlockSpec((1, n, width), lambda bi, i: (bi, 0, 0))
    return pl.pallas_call(
        functools.partial(_mem_attn_kernel, heads=heads, hd=hd),
        grid=(b, s // tm),
        in_specs=[pl.BlockSpec((1, tm, width), lambda bi, i: (bi, i, qblock)), kv, kv,
                  pl.BlockSpec((1, hd), lambda bi, i: (0, 0))],
        out_specs=pl.BlockSpec((1, tm, width), lambda bi, i: (bi, i, 0)),
        out_shape=jax.ShapeDtypeStruct((b, s, width), out_dtype),
        compiler_params=_params("parallel", "parallel"),
    )(q3, mk, mv, g.reshape(1, hd))


def _head_norm_kernel(x_ref, g_ref, o_ref, *, heads, hd):
    for h in range(heads):
        sl = slice(h * hd, (h + 1) * hd)
        x = x_ref[:, sl]
        o_ref[:, sl] = x * lax.rsqrt(jnp.mean(x * x, axis=-1, keepdims=True) + RMS_EPS) * g_ref[...]


def _head_norm(x, g, *, heads, hd):
    m, width = x.shape
    tm = _pick(m, 512, SUBLANES)
    return pl.pallas_call(
        functools.partial(_head_norm_kernel, heads=heads, hd=hd),
        grid=(m // tm,),
        in_specs=[pl.BlockSpec((tm, width), lambda i: (i, 0)), pl.BlockSpec((1, hd), lambda i: (0, 0))],
        out_specs=pl.BlockSpec((tm, width), lambda i: (i, 0)),
        out_shape=jax.ShapeDtypeStruct((m, width), F32),
        compiler_params=_params("parallel"),
    )(x, g.reshape(1, hd))


def _ffn_activation(g, u, ext_ref, w_ref, b_ref, tm):
    ext_ref[SUBLANES:SUBLANES + tm, :] = g
    w = w_ref[...]
    gc = w[0:1] * ext_ref[SUBLANES - 2:SUBLANES - 2 + tm, :]
    gc = gc + w[1:2] * ext_ref[SUBLANES - 1:SUBLANES - 1 + tm, :]
    gc = gc + w[2:3] * g
    gc = gc + b_ref[...]
    return gc * (1.0 / (1.0 + jnp.exp(-gc))) * u


def _ffn_fused_kernel(x_ref, wg_ref, wu_ref, cw_ref, cb_ref, hist_ref, a_ref, st_ref, tail_ref, *, s, rc):
    tail_ref[SUBLANES - HIST:SUBLANES, :] = hist_ref[0]
    w = cw_ref[...]
    bias = cb_ref[...]
    tn = w.shape[1]
    row = lax.broadcasted_iota(jnp.int32, (SUBLANES, tn), 0)
    for c in range(s // rc):
        r0 = c * rc
        x = x_ref[0, r0:r0 + rc, :]
        g = jnp.dot(x, wg_ref[...], preferred_element_type=F32)
        u = jnp.dot(x, wu_ref[...], preferred_element_type=F32)
        p2 = tail_ref[SUBLANES - 2:SUBLANES - 1, :]
        p1 = tail_ref[SUBLANES - 1:SUBLANES, :]
        g1 = pltpu.roll(g, 1, axis=0)
        g2 = pltpu.roll(g, 2, axis=0)
        h1 = jnp.where(row == 0, p1, g1[0:SUBLANES])
        h2 = jnp.where(row == 0, p2, jnp.where(row == 1, p1, g2[0:SUBLANES]))
        g1 = jnp.concatenate([h1, g1[SUBLANES:]], axis=0)
        g2 = jnp.concatenate([h2, g2[SUBLANES:]], axis=0)
        tail_ref[...] = g[rc - SUBLANES:rc]
        gc = w[0:1] * g2 + w[1:2] * g1 + w[2:3] * g + bias
        a_ref[0, r0:r0 + rc, :] = (gc * (1.0 / (1.0 + jnp.exp(-gc))) * u).astype(a_ref.dtype)
    st_ref[0] = tail_ref[SUBLANES - HIST:SUBLANES, :]


def _ffn_fused(h3, wg, wu, layer, cw, cb, hist, *, tn=256, rc=256):
    b, s, d = h3.shape
    f = wg.shape[2]
    tn = _pick(f, tn, LANES)
    rc = _pick(s, rc, SUBLANES)
    wspec = pl.BlockSpec((None, d, tn), lambda bi, j: (layer, 0, j))
    hspec = pl.BlockSpec((1, HIST, tn), lambda bi, j: (bi, 0, j))
    return pl.pallas_call(
        functools.partial(_ffn_fused_kernel, s=s, rc=rc),
        grid=(b, f // tn),
        in_specs=[pl.BlockSpec((1, s, d), lambda bi, j: (bi, 0, 0)), wspec, wspec,
                  pl.BlockSpec((CONV_W, tn), lambda bi, j: (0, j)),
                  pl.BlockSpec((1, tn), lambda bi, j: (0, j)), hspec],
        out_specs=[pl.BlockSpec((1, s, tn), lambda bi, j: (bi, 0, j)), hspec],
        out_shape=[jax.ShapeDtypeStruct((b, s, f), BF16), jax.ShapeDtypeStruct((b, HIST, f), F32)],
        scratch_shapes=[pltpu.VMEM((SUBLANES, tn), F32)],
        compiler_params=_params("parallel", "parallel"),
    )(h3, wg, wu, cw, cb.reshape(1, f), hist)


def _ffn_act_kernel(g_ref, u_ref, cw_ref, cb_ref, hist_ref, a_ref, st_ref, ext_ref, *, tm):
    ext_ref[SUBLANES - HIST:SUBLANES, :] = hist_ref[0]
    a_ref[0] = _ffn_activation(g_ref[0], u_ref[0], ext_ref, cw_ref, cb_ref, tm).astype(a_ref.dtype)
    st_ref[0] = ext_ref[tm + SUBLANES - HIST:tm + SUBLANES, :]


def _ffn_act(g3, u3, cw, cb, hist, *, tn=1024):
    b, t, f = g3.shape
    tn = _pick(f, tn, LANES)
    blk = pl.BlockSpec((1, t, tn), lambda bi, j: (bi, 0, j))
    hspec = pl.BlockSpec((1, HIST, tn), lambda bi, j: (bi, 0, j))
    return pl.pallas_call(
        functools.partial(_ffn_act_kernel, tm=t),
        grid=(b, f // tn),
        in_specs=[blk, blk, pl.BlockSpec((CONV_W, tn), lambda bi, j: (0, j)),
                  pl.BlockSpec((1, tn), lambda bi, j: (0, j)), hspec],
        out_specs=[blk, hspec],
        out_shape=[jax.ShapeDtypeStruct((b, t, f), F32), jax.ShapeDtypeStruct((b, HIST, f), F32)],
        scratch_shapes=[pltpu.VMEM((t + SUBLANES, tn), F32)],
        compiler_params=_params("parallel", "parallel"),
    )(g3, u3, cw, cb.reshape(1, f), hist)


def kernel(x_prompt, x_sample, cache_k, cache_v, cache_mem_k, cache_mem_v, state_conv, state_ffn, page_table, mem_prompt, norm_mix, w_in, q_norm, k_norm, lambda_q1, lambda_k1, lambda_q2, lambda_k2, subln, sc_conv_w, mq_norm, norm_mem, w_mem_k, w_mem_v, mk_norm, w_out, norm_ffn, w_gate, w_up, ffn_conv_w, ffn_conv_b, w_down):
    b, s, d = x_prompt.shape
    bd, t, _ = x_sample.shape
    depth, n_pool, page, heads, _, dqk = cache_k.shape
    dv = cache_v.shape[-1]
    qcols = heads * 2 * dqk
    da_width = heads * dv
    sc_width = sc_conv_w.shape[-1]
    n_mem, mem_heads, mem_hd = cache_mem_k.shape[2:]
    mem_width = mem_heads * mem_hd
    d_ff = w_gate.shape[-1]
    n_pages = page_table.shape[1]
    past_len = n_pages * page
    c2 = 2 * qcols + da_width
    c5 = c2 + 3 * sc_width
    assert qcols == da_width and dv == 2 * dqk
    assert c2 % sc_width == 0 and c5 % mem_width == 0 and (2 * qcols) % dv == 0
    assert t <= page and (2 * heads * t) % SUBLANES == 0
    sc_block0, mq_block, v_block0 = c2 // sc_width, c5 // mem_width, (2 * qcols) // dv

    cos_p, sin_p = _rope_tables(jnp.arange(s, dtype=jnp.int32), dqk)
    cos_s, sin_s = _rope_tables(past_len + jnp.arange(t, dtype=jnp.int32), dqk)
    t_mq = 16
    assert t <= t_mq
    new_cols = -(-t * heads // LANES) * LANES

    xp = x_prompt.reshape(b * s, d)
    xs = x_sample.reshape(bd * t, d)
    mem2 = mem_prompt.reshape(b * n_mem, d)
    zeros_conv = jnp.zeros((b, HIST, sc_width), F32)
    zeros_ffn = jnp.zeros((b, HIST, d_ff), F32)
    w_mem_k_b, w_mem_v_b = w_mem_k.astype(BF16), w_mem_v.astype(BF16)

    outs = {k: [] for k in ("kp", "vp", "cp", "fp", "mkp", "mvp", "ks", "vs", "cs", "fs")}
    for l in range(depth):
        lam_init = _lambda_init(l)
        lams = (lambda_q1[l], lambda_k1[l], lambda_q2[l], lambda_k2[l])

        proj, w_in_b = _matmul_cast(_rmsnorm(xs, norm_mix[l]), w_in, l)
        proj3 = proj.reshape(bd, t, -1)
        qb, kf, _ = _qk_rope(proj3, cos_s, sin_s, q_norm[l], k_norm[l], qcols=qcols, dqk=dqk, qscale=dqk ** -0.5)
        v_new = proj3[:, :, 2 * qcols:c2]
        qs = qb.reshape(bd, t, heads, 2, dqk).transpose(0, 3, 2, 1, 4).reshape(bd, 2, heads * t, dqk)
        kn = kf.reshape(bd, t, heads, 2, dqk).transpose(0, 3, 1, 2, 4).reshape(bd, 2, t * heads, dqk)
        kn = jnp.pad(kn, ((0, 0), (0, 0), (0, new_cols - t * heads), (0, 0)))
        vn = jnp.pad(v_new.reshape(bd, t * heads, dv), ((0, 0), (0, new_cols - t * heads), (0, 0)))
        da = _attn_paged(page_table, qs, cache_k, cache_v, kn, vn, lams, subln[l], layer=l, t_new=t,
                         lam_init=lam_init)
        da = da.reshape(bd, heads, t, dv).transpose(0, 2, 1, 3).reshape(bd, t, da_width)
        sc, conv_state = _sconv(proj3, sc_conv_w[l], state_conv[l], width=sc_width, block0=sc_block0, out_dtype=F32)
        mq = jnp.pad(proj3[:, :, c5:], ((0, 0), (0, t_mq - t), (0, 0)))
        mo = _mem_attn(mq, cache_mem_k[l].reshape(bd, n_mem, mem_width), cache_mem_v[l].reshape(bd, n_mem, mem_width),
                       mq_norm[l], heads=mem_heads, hd=mem_hd, qblock=0, out_dtype=F32)[:, :t]
        mix = jnp.concatenate([da, sc, mo], axis=-1).astype(BF16).reshape(bd * t, -1)
        xs, w_out_b = _matmul_cast(mix, w_out, l, xs)
        h2 = _rmsnorm(xs, norm_ffn[l])
        g2, w_gate_b = _matmul_cast(h2, w_gate, l)
        u2, w_up_b = _matmul_cast(h2, w_up, l)
        act, ffn_state = _ffn_act(g2.reshape(bd, t, d_ff), u2.reshape(bd, t, d_ff), ffn_conv_w[l], ffn_conv_b[l],
                                  state_ffn[l], tn=d_ff // 2)
        xs, w_down_b = _matmul_cast(act.astype(BF16).reshape(bd * t, d_ff), w_down, l, xs)

        outs["ks"].append(kf.reshape(bd, t, heads, 2, dqk))
        outs["vs"].append(v_new.reshape(bd, t, heads, dv))
        outs["cs"].append(conv_state)
        outs["fs"].append(ffn_state)

        hm = _rmsnorm(mem2, norm_mem[l])
        mk = _head_norm(_matmul(hm, w_mem_k_b, l), mk_norm[l], heads=mem_heads, hd=mem_hd)
        mv = _matmul(hm, w_mem_v_b, l)
        mk3 = mk.reshape(b, n_mem, mem_width)
        mv3 = mv.reshape(b, n_mem, mem_width)

        proj3 = _matmul(_rmsnorm(xp, norm_mix[l]), w_in_b, 0).reshape(b, s, -1)
        qb, kf, kb = _qk_rope(proj3, cos_p, sin_p, q_norm[l], k_norm[l], qcols=qcols, dqk=dqk,
                              qscale=dqk ** -0.5 * math.log2(math.e))
        da = _attn_prompt(qb, kb, proj3, lams, subln[l], heads=heads, dqk=dqk, dv=dv, vblock0=v_block0,
                          lam_init=lam_init)
        sc, conv_state = _sconv(proj3, sc_conv_w[l], zeros_conv, width=sc_width, block0=sc_block0, out_dtype=BF16)
        mo = _mem_attn(proj3, mk3, mv3, mq_norm[l], heads=mem_heads, hd=mem_hd, qblock=mq_block, out_dtype=BF16)
        xp = _matmul([da.reshape(b * s, -1), sc.reshape(b * s, -1), mo.reshape(b * s, -1)], w_out_b, 0, xp)
        h2 = _rmsnorm(xp, norm_ffn[l]).reshape(b, s, d)
        act, ffn_state = _ffn_fused(h2, w_gate_b, w_up_b, 0, ffn_conv_w[l], ffn_conv_b[l], zeros_ffn)
        xp = _matmul(act.reshape(b * s, d_ff), w_down_b, 0, xp, tm=512, tn=512)

        outs["kp"].append(kf.reshape(b, s, heads, 2, dqk))
        outs["vp"].append(proj3[:, :, 2 * qcols:c2].reshape(b, s, heads, dv))
        outs["cp"].append(conv_state)
        outs["fp"].append(ffn_state)
        outs["mkp"].append(mk.reshape(b, n_mem, mem_heads, mem_hd))
        outs["mvp"].append(mv.reshape(b, n_mem, mem_heads, mem_hd))

    st = lambda k: jnp.stack(outs[k])
    return (xp.reshape(b, s, d), xs.reshape(bd, t, d),
            st("kp"), st("vp"), st("cp"), st("fp"), st("mkp"), st("mvp"),
            st("ks"), st("vs"), st("cs"), st("fs"))
```

```python
import functools
import math

import jax
import jax.numpy as jnp
from jax import lax
from jax.experimental import pallas as pl
from jax.experimental.pallas import tpu as pltpu

ROPE_THETA = 10000.0
RMS_EPS = 1e-6
NEG_INF = -1e30
CONV_W = 3
HIST = CONV_W - 1
V7X_VMEM_BYTES = 64 * 1024 * 1024
VMEM_LIMIT = V7X_VMEM_BYTES - 8 * 1024 * 1024
SUBLANES = 8
LANES = 128

F32 = jnp.float32
BF16 = jnp.bfloat16


def _params(*sem):
    return pltpu.CompilerParams(dimension_semantics=sem, vmem_limit_bytes=VMEM_LIMIT)


def _pick(n, pref, align):
    if n <= pref:
        return n
    t = (pref // align) * align
    while t >= align:
        if n % t == 0:
            return t
        t -= align
    return n


def _lambda_init(l):
    return 0.8 - 0.6 * math.exp(-0.3 * l)


def _rmsnorm_kernel(x_ref, g_ref, o_ref):
    x = x_ref[...]
    ms = jnp.mean(x * x, axis=-1, keepdims=True)
    o_ref[...] = (x * lax.rsqrt(ms + RMS_EPS) * g_ref[...]).astype(o_ref.dtype)


def _rmsnorm(x, g):
    m, d = x.shape
    tm = _pick(m, 256, SUBLANES)
    return pl.pallas_call(
        _rmsnorm_kernel,
        grid=(m // tm,),
        in_specs=[pl.BlockSpec((tm, d), lambda i: (i, 0)),
                  pl.BlockSpec((1, d), lambda i: (0, 0))],
        out_specs=pl.BlockSpec((tm, d), lambda i: (i, 0)),
        out_shape=jax.ShapeDtypeStruct((m, d), BF16),
        compiler_params=_params("parallel"),
    )(x, g.reshape(1, d))


def _matmul_kernel(*refs, nparts, nk, has_res):
    xs, ws = refs[:nparts], refs[nparts:2 * nparts]
    r_ref = refs[2 * nparts] if has_res else None
    o_ref = refs[-1]
    d = None
    for x_ref, w_ref in zip(xs, ws):
        part = jnp.dot(x_ref[...], w_ref[...], preferred_element_type=F32)
        d = part if d is None else d + part
    if nk == 1:
        o_ref[...] = r_ref[...] + d if has_res else d
    else:
        k = pl.program_id(2)

        @pl.when(k == 0)
        def _():
            o_ref[...] = r_ref[...] + d if has_res else d

        @pl.when(k > 0)
        def _():
            o_ref[...] += d


def _matmul(xs, w, layer, res=None, *, tm=1024, tn=1024, tk=None):
    xs = list(xs) if isinstance(xs, (list, tuple)) else [xs]
    m = xs[0].shape[0]
    kdim, n = w.shape[1], w.shape[2]
    tm = _pick(m, tm, SUBLANES)
    tn = _pick(n, tn, LANES)
    nk = 1 if tk is None else kdim // tk
    assert nk == 1 or len(xs) == 1
    in_specs, w_specs, off = [], [], 0
    for x in xs:
        kp = x.shape[1] if nk == 1 else tk
        assert off % kp == 0
        in_specs.append(pl.BlockSpec((tm, kp), lambda i, j, k: (i, k)))
        w_specs.append(pl.BlockSpec((None, kp, tn), lambda i, j, k, r=off // kp: (layer, r + k, j)))
        off += x.shape[1]
    assert off == kdim
    in_specs += w_specs
    args = xs + [w] * len(xs)
    if res is not None:
        in_specs.append(pl.BlockSpec((tm, tn), lambda i, j, k: (i, j)))
        args.append(res)
    return pl.pallas_call(
        functools.partial(_matmul_kernel, nparts=len(xs), nk=nk, has_res=res is not None),
        grid=(m // tm, n // tn, nk),
        in_specs=in_specs,
        out_specs=pl.BlockSpec((tm, tn), lambda i, j, k: (i, j)),
        out_shape=jax.ShapeDtypeStruct((m, n), F32),
        compiler_params=_params("parallel", "parallel", "arbitrary"),
    )(*args)


def _matmul_cast_kernel(*refs, has_res):
    if has_res:
        x_ref, w_ref, r_ref, o_ref, wb_ref = refs
    else:
        x_ref, w_ref, o_ref, wb_ref = refs
    wb = w_ref[...].astype(BF16)
    wb_ref[...] = wb
    d = jnp.dot(x_ref[...], wb, preferred_element_type=F32)
    o_ref[...] = r_ref[...] + d if has_res else d


def _matmul_cast(x, w, layer, res=None):
    m, kdim = x.shape
    n = w.shape[2]
    tn = _pick(n, max(LANES, (8 * 1024 * 1024) // (4 * kdim) // LANES * LANES), LANES)
    in_specs = [pl.BlockSpec((m, kdim), lambda j: (0, 0)),
                pl.BlockSpec((None, kdim, tn), lambda j: (layer, 0, j))]
    args = [x, w]
    if res is not None:
        in_specs.append(pl.BlockSpec((m, tn), lambda j: (0, j)))
        args.append(res)
    return pl.pallas_call(
        functools.partial(_matmul_cast_kernel, has_res=res is not None),
        grid=(n // tn,),
        in_specs=in_specs,
        out_specs=[pl.BlockSpec((m, tn), lambda j: (0, j)), pl.BlockSpec((None, kdim, tn), lambda j: (0, 0, j))],
        out_shape=[jax.ShapeDtypeStruct((m, n), F32), jax.ShapeDtypeStruct((1, kdim, n), BF16)],
        compiler_params=_params("parallel"),
    )(*args)


def _matmul_cast_pair_kernel(x_ref, wa_ref, wb_ref, oa_ref, ob_ref, wab_ref, *, tn):
    wab = jnp.concatenate([wa_ref[...].astype(BF16), wb_ref[...].astype(BF16)], axis=1)
    wab_ref[...] = wab
    d = jnp.dot(x_ref[...], wab, preferred_element_type=F32)
    oa_ref[...] = d[:, :tn]
    ob_ref[...] = d[:, tn:]


def _matmul_cast_pair(x, wa, wb, layer, *, tn):
    m, kdim = x.shape
    n = wa.shape[2]
    wspec = pl.BlockSpec((None, kdim, tn), lambda j: (layer, 0, j))
    ospec = pl.BlockSpec((m, tn), lambda j: (0, j))
    return pl.pallas_call(
        functools.partial(_matmul_cast_pair_kernel, tn=tn),
        grid=(n // tn,),
        in_specs=[pl.BlockSpec((m, kdim), lambda j: (0, 0)), wspec, wspec],
        out_specs=[ospec, ospec, pl.BlockSpec((None, kdim, 2 * tn), lambda j: (0, 0, j))],
        out_shape=[jax.ShapeDtypeStruct((m, n), F32), jax.ShapeDtypeStruct((m, n), F32),
                   jax.ShapeDtypeStruct((1, kdim, 2 * n), BF16)],
        compiler_params=_params("parallel"),
    )(x, wa, wb)


def _qk_rope_kernel(q_ref, k_ref, cos_ref, sin_ref, qg_ref, kg_ref, qo_ref, ko_ref, kb_ref, *, nsub, dqk, qscale):
    c = cos_ref[...]
    s = sin_ref[...]

    def norm_rope(x, g):
        y = x * lax.rsqrt(jnp.mean(x * x, axis=-1, keepdims=True) + RMS_EPS) * g
        return y * c + pltpu.roll(y, dqk // 2, axis=1) * s

    for idx in range(nsub):
        sl = slice(idx * dqk, (idx + 1) * dqk)
        qo_ref[0, :, sl] = (norm_rope(q_ref[0, :, sl], qg_ref[...]) * qscale).astype(qo_ref.dtype)
        kr = norm_rope(k_ref[0, :, sl], kg_ref[...])
        ko_ref[0, :, sl] = kr
        kb_ref[0, :, sl] = kr.astype(kb_ref.dtype)


def _qk_rope(proj3, cos, sin, qg, kg, *, qcols, dqk, qscale):
    b, s, _ = proj3.shape
    tm = _pick(s, 256, SUBLANES)
    blk = lambda c: pl.BlockSpec((1, tm, qcols), lambda bi, i, c=c: (bi, i, c))
    tab = pl.BlockSpec((tm, dqk), lambda bi, i: (i, 0))
    gain = pl.BlockSpec((1, dqk), lambda bi, i: (0, 0))
    out = pl.BlockSpec((1, tm, qcols), lambda bi, i: (bi, i, 0))
    return pl.pallas_call(
        functools.partial(_qk_rope_kernel, nsub=qcols // dqk, dqk=dqk, qscale=qscale),
        grid=(b, s // tm),
        in_specs=[blk(0), blk(1), tab, tab, gain, gain],
        out_specs=[out, out, out],
        out_shape=[jax.ShapeDtypeStruct((b, s, qcols), BF16),
                   jax.ShapeDtypeStruct((b, s, qcols), F32),
                   jax.ShapeDtypeStruct((b, s, qcols), BF16)],
        compiler_params=_params("parallel", "parallel"),
    )(proj3, proj3, cos, sin, qg.reshape(1, dqk), kg.reshape(1, dqk))


def _rope_tables(pos, dqk):
    half = dqk // 2
    inv = ROPE_THETA ** (-jnp.arange(half, dtype=F32) / half)
    ang = pos.astype(F32)[:, None] * inv[None, :]
    cos, sin = jnp.cos(ang), jnp.sin(ang)
    return jnp.concatenate([cos, cos], axis=-1), jnp.concatenate([-sin, sin], axis=-1)


def _lambda_value(lq1, lk1, lq2, lk2, lam_init):
    a = jnp.sum(lq1[...] * lk1[...], axis=-1, keepdims=True)
    b = jnp.sum(lq2[...] * lk2[...], axis=-1, keepdims=True)
    return jnp.exp(a) - jnp.exp(b) + lam_init


def _diff_combine(o0, o1, lam, subln, lam_init):
    da = o0 - lam * o1
    y = da * lax.rsqrt(jnp.mean(da * da, axis=-1, keepdims=True) + RMS_EPS) * subln
    return y * (1.0 - lam_init)


def _attn_prompt_kernel(qi_ref, kj_ref, q_ref, k_ref, v_ref, lq1, lk1, lq2, lk2, sub_ref, o_ref, m_ref, l_ref, acc_ref,
                        *, t, rc, dqk, dv, lam_init):
    step = pl.program_id(2)
    i = qi_ref[step]
    j = kj_ref[step]

    @pl.when(j == 0)
    def _():
        m_ref[...] = jnp.full(m_ref.shape, NEG_INF, F32)
        l_ref[...] = jnp.zeros(l_ref.shape, F32)
        acc_ref[...] = jnp.zeros(acc_ref.shape, F32)

    def block(diagonal):
        vb = v_ref[0].astype(BF16)
        for c in range(t // rc):
            rows = slice(c * rc, (c + 1) * rc)
            kv = (c + 1) * rc if diagonal else t
            for mp in range(2):
                sl = slice(mp * dqk, (mp + 1) * dqk)
                s = lax.dot_general(q_ref[0, rows, sl], k_ref[0, 0:kv, sl], (((1,), (1,)), ((), ())),
                                    preferred_element_type=F32)
                if diagonal:
                    qpos = lax.broadcasted_iota(jnp.int32, (rc, kv), 0) + c * rc
                    kpos = lax.broadcasted_iota(jnp.int32, (rc, kv), 1)
                    s = jnp.where(kpos <= qpos, s, NEG_INF)
                m_prev = m_ref[mp, rows]
                m_new = jnp.maximum(m_prev, jnp.max(s, axis=-1, keepdims=True))
                corr = jnp.exp2(m_prev - m_new)
                p = jnp.exp2(s - pltpu.repeat(m_new, kv // LANES, axis=1))
                l_ref[mp, rows] = l_ref[mp, rows] * corr + jnp.sum(p, axis=-1, keepdims=True)
                acc_ref[mp, rows] = (acc_ref[mp, rows] * pltpu.repeat(corr, dv // LANES, axis=1)
                                     + jnp.dot(p.astype(BF16), vb[0:kv], preferred_element_type=F32))
                m_ref[mp, rows] = m_new

    @pl.when(j < i)
    def _():
        block(False)

    @pl.when(j == i)
    def _():
        block(True)
        lam = _lambda_value(lq1, lk1, lq2, lk2, lam_init)
        o0 = acc_ref[0] / pltpu.repeat(l_ref[0], dv // LANES, axis=1)
        o1 = acc_ref[1] / pltpu.repeat(l_ref[1], dv // LANES, axis=1)
        o_ref[0] = _diff_combine(o0, o1, lam, sub_ref[...], lam_init).astype(o_ref.dtype)


def _attn_prompt(qb, kb, proj3, lams, subln, *, heads, dqk, dv, vblock0, lam_init):
    b, s, _ = qb.shape
    t = _pick(s, 1024, LANES)
    n = s // t
    pairs =[(i, j) for i in range(n) for j in range(i + 1)]
    qi = jnp.asarray([p[0] for p in pairs], jnp.int32)
    kj = jnp.asarray([p[1] for p in pairs], jnp.int32)
    qspec = pl.BlockSpec((1, t, 2 * dqk), lambda bi, h, st, qi, kj: (bi, qi[st], h))
    kspec = pl.BlockSpec((1, t, 2 * dqk), lambda bi, h, st, qi, kj: (bi, kj[st], h))
    vspec = pl.BlockSpec((1, t, dv), lambda bi, h, st, qi, kj: (bi, kj[st], vblock0 + h))
    vec = lambda w: pl.BlockSpec((1, w), lambda bi, h, st, qi, kj: (0, 0))
    grid_spec = pltpu.PrefetchScalarGridSpec(
        num_scalar_prefetch=2,
        grid=(b, heads, len(pairs)),
        in_specs=[qspec, kspec, vspec, vec(dqk), vec(dqk), vec(dqk), vec(dqk), vec(dv)],
        out_specs=pl.BlockSpec((1, t, dv), lambda bi, h, st, qi, kj: (bi, qi[st], h)),
        scratch_shapes=[pltpu.VMEM((2, t, LANES), F32), pltpu.VMEM((2, t, LANES), F32), pltpu.VMEM((2, t, dv), F32)],
    )
    return pl.pallas_call(
        functools.partial(_attn_prompt_kernel, t=t, rc=min(t, 4 * LANES), dqk=dqk, dv=dv, lam_init=lam_init),
        grid_spec=grid_spec,
        out_shape=jax.ShapeDtypeStruct((b, s, heads * dv), BF16),
        compiler_params=_params("parallel", "parallel", "arbitrary"),
    )(qi, kj, qb, kb, proj3, *[x.reshape(1, dqk) for x in lams], subln.reshape(1, dv))


PAGES_PER_STEP = 8


def _attn_paged_kernel(*refs, t_new, heads, npp, lam_init):
    pt_ref, q_ref = refs[0], refs[1]
    k_refs, v_refs = refs[2:2 + npp], refs[2 + npp:2 + 2 * npp]
    kn_ref, vn_ref, lq1, lk1, lq2, lk2, sub_ref, o_ref, m_ref, l_ref, acc_ref = refs[2 + 2 * npp:]
    del pt_ref
    p = pl.program_id(1)
    hq = heads * t_new
    kcols = v_refs[0].shape[0]

    @pl.when(p == 0)
    def _():
        m_ref[...] = jnp.full(m_ref.shape, NEG_INF, F32)
        l_ref[...] = jnp.zeros(l_ref.shape, F32)
        acc_ref[...] = jnp.zeros(acc_ref.shape, F32)

    def scores(kmats, ncols):
        parts = [lax.dot_general(q_ref[0, mp], kmats[mp].astype(BF16), (((1,), (1,)), ((), ())),
                                 preferred_element_type=F32) for mp in range(2)]
        s = jnp.concatenate(parts, axis=0)
        qrow = lax.broadcasted_iota(jnp.int32, (2 * hq, ncols), 0) % hq
        col = lax.broadcasted_iota(jnp.int32, (2 * hq, ncols), 1)
        return s, qrow, col

    def update(s_list, v_list):
        m_prev = m_ref[...]
        m_new = m_prev
        for s in s_list:
            m_new = jnp.maximum(m_new, jnp.max(s, axis=-1, keepdims=True))
        corr = jnp.exp(m_prev - m_new)
        l_new = l_ref[...] * corr
        acc = acc_ref[...] * corr
        for s, v in zip(s_list, v_list):
            pm = jnp.exp(s - m_new)
            l_new = l_new + jnp.sum(pm, axis=-1, keepdims=True)
            acc = acc + jnp.dot(pm.astype(BF16), v.astype(BF16), preferred_element_type=F32)
        l_ref[...] = l_new
        acc_ref[...] = acc
        m_ref[...] = m_new

    s_list = []
    for r in range(npp):
        kmats = [k_refs[r][pl.ds(mp, kcols, stride=2), :] for mp in range(2)]
        s, qrow, col = scores(kmats, kcols)
        s_list.append(jnp.where(col % heads == qrow // t_new, s, NEG_INF))
    update(s_list, [v_refs[r][...] for r in range(npp)])

    @pl.when(p == pl.num_programs(1) - 1)
    def _():
        ncols = vn_ref.shape[1]
        s, qrow, col = scores([kn_ref[0, 0], kn_ref[0, 1]], ncols)
        keep = (col % heads == qrow // t_new) & (col // heads <= qrow % t_new)
        update([jnp.where(keep, s, NEG_INF)], [vn_ref[0]])
        lam = _lambda_value(lq1, lk1, lq2, lk2, lam_init)
        o = acc_ref[...] / l_ref[...]
        o_ref[0] = _diff_combine(o[0:hq], o[hq:2 * hq], lam, sub_ref[...], lam_init)


def _attn_paged(page_table, qs, cache_k, cache_v, k_new, v_new, lams, subln, *, layer, t_new, lam_init):
    bd, n_pages = page_table.shape
    depth, n_pool, page, heads, _, dqk = cache_k.shape
    dv = cache_v.shape[-1]
    npp = PAGES_PER_STEP if n_pages % PAGES_PER_STEP == 0 else 1
    hq = heads * t_new
    ck = cache_k.reshape(depth, n_pool, page * heads * 2, dqk)
    cv = cache_v.reshape(depth, n_pool, page * heads, dv)
    ncols = v_new.shape[1]

    def cspec(r, rows, width):
        return pl.BlockSpec((None, None, rows, width), lambda b, p, pt, r=r: (layer, pt[b, p * npp + r], 0, 0))

    vec = lambda w: pl.BlockSpec((1, w), lambda b, p, pt: (0, 0))
    grid_spec = pltpu.PrefetchScalarGridSpec(
        num_scalar_prefetch=1,
        grid=(bd, n_pages // npp),
        in_specs=[pl.BlockSpec((1, 2, hq, dqk), lambda b, p, pt: (b, 0, 0, 0))]
        + [cspec(r, page * heads * 2, dqk) for r in range(npp)]
        + [cspec(r, page * heads, dv) for r in range(npp)]
        + [pl.BlockSpec((1, 2, ncols, dqk), lambda b, p, pt: (b, 0, 0, 0)),
           pl.BlockSpec((1, ncols, dv), lambda b, p, pt: (b, 0, 0)),
           vec(dqk), vec(dqk), vec(dqk), vec(dqk), vec(dv)],
        out_specs=pl.BlockSpec((1, hq, dv), lambda b, p, pt: (b, 0, 0)),
        scratch_shapes=[pltpu.VMEM((2 * hq, 1), F32), pltpu.VMEM((2 * hq, 1), F32), pltpu.VMEM((2 * hq, dv), F32)],
    )
    return pl.pallas_call(
        functools.partial(_attn_paged_kernel, t_new=t_new, heads=heads, npp=npp, lam_init=lam_init),
        grid_spec=grid_spec,
        out_shape=jax.ShapeDtypeStruct((bd, hq, dv), F32),
        compiler_params=_params("parallel", "arbitrary"),
    )(page_table, qs, *([ck] * npp), *([cv] * npp), k_new, v_new,
      *[x.reshape(1, dqk) for x in lams], subln.reshape(1, dv))


def _sconv_kernel(gb_ref, gc_ref, hc_ref, w_ref, hist_ref, o_ref, st_ref, ext_ref, *, tm):
    i = pl.program_id(1)

    @pl.when(i == 0)
    def _():
        ext_ref[SUBLANES - HIST:SUBLANES, :] = hist_ref[0]

    u = gc_ref[0] * hc_ref[0]
    ext_ref[SUBLANES:SUBLANES + tm, :] = u
    w = w_ref[...]
    cu = w[0:1] * ext_ref[SUBLANES - 2:SUBLANES - 2 + tm, :]
    cu = cu + w[1:2] * ext_ref[SUBLANES - 1:SUBLANES - 1 + tm, :]
    cu = cu + w[2:3] * u
    o_ref[0] = (gb_ref[0] * cu).astype(o_ref.dtype)
    last = ext_ref[tm + SUBLANES - HIST:tm + SUBLANES, :]
    st_ref[0] = last
    ext_ref[SUBLANES - HIST:SUBLANES, :] = last


def _sconv(proj3, w, hist, *, width, block0, out_dtype):
    b, s, _ = proj3.shape
    tm = _pick(s, 512, SUBLANES)
    blk = lambda c: pl.BlockSpec((1, tm, width), lambda bi, i, c=c: (bi, i, block0 + c))
    return pl.pallas_call(
        functools.partial(_sconv_kernel, tm=tm),
        grid=(b, s // tm),
        in_specs=[blk(0), blk(1), blk(2),
                  pl.BlockSpec((CONV_W, width), lambda bi, i: (0, 0)),
                  pl.BlockSpec((1, HIST, width), lambda bi, i: (bi, 0, 0))],
        out_specs=[pl.BlockSpec((1, tm, width), lambda bi, i: (bi, i, 0)),
                   pl.BlockSpec((1, HIST, width), lambda bi, i: (bi, 0, 0))],
        out_shape=[jax.ShapeDtypeStruct((b, s, width), out_dtype),
                   jax.ShapeDtypeStruct((b, HIST, width), F32)],
        scratch_shapes=[pltpu.VMEM((tm + SUBLANES, width), F32)],
        compiler_params=_params("parallel", "arbitrary"),
    )(proj3, proj3, proj3, w, hist)


def _mem_attn_kernel(q_ref, k_ref, v_ref, g_ref, o_ref, *, heads, hd):
    scale = hd ** -0.5
    for h in range(heads):
        sl = slice(h * hd, (h + 1) * hd)
        q = q_ref[0, :, sl]
        qn = q * lax.rsqrt(jnp.mean(q * q, axis=-1, keepdims=True) + RMS_EPS) * g_ref[...]
        s = lax.dot_general((qn * scale).astype(BF16), k_ref[0, :, sl].astype(BF16), (((1,), (1,)), ((), ())),
                            preferred_element_type=F32)
        p = jnp.exp(s - jnp.max(s, axis=-1, keepdims=True))
        o = jnp.dot(p.astype(BF16), v_ref[0, :, sl].astype(BF16), preferred_element_type=F32)
        o_ref[0, :, sl] = (o / jnp.sum(p, axis=-1, keepdims=True)).astype(o_ref.dtype)


def _mem_attn(q3, mk, mv, layer, g, *, heads, hd, qblock, out_dtype):
    b, s, _ = q3.shape
    n = mk.shape[2]
    width = heads * hd
    tm = _pick(s, 512, SUBLANES)
    kv = pl.BlockSpec((None, 1, n, width), lambda bi, i: (layer, bi, 0, 0))
    return pl.pallas_call(
        functools.partial(_mem_attn_kernel, heads=heads, hd=hd),
        grid=(b, s // tm),
        in_specs=[pl.BlockSpec((1, tm, width), lambda bi, i: (bi, i, qblock)), kv, kv,
                  pl.BlockSpec((1, hd), lambda bi, i: (0, 0))],
        out_specs=pl.BlockSpec((1, tm, width), lambda bi, i: (bi, i, 0)),
        out_shape=jax.ShapeDtypeStruct((b, s, width), out_dtype),
        compiler_params=_params("parallel", "parallel"),
    )(q3, mk, mv, g.reshape(1, hd))


def _head_norm_kernel(x_ref, g_ref, o_ref, *, heads, hd):
    for h in range(heads):
        sl = slice(h * hd, (h + 1) * hd)
        x = x_ref[:, sl]
        o_ref[:, sl] = x * lax.rsqrt(jnp.mean(x * x, axis=-1, keepdims=True) + RMS_EPS) * g_ref[...]


def _head_norm(x, g, *, heads, hd):
    m, width = x.shape
    tm = _pick(m, 512, SUBLANES)
    return pl.pallas_call(
        functools.partial(_head_norm_kernel, heads=heads, hd=hd),
        grid=(m // tm,),
        in_specs=[pl.BlockSpec((tm, width), lambda i: (i, 0)), pl.BlockSpec((1, hd), lambda i: (0, 0))],
        out_specs=pl.BlockSpec((tm, width), lambda i: (i, 0)),
        out_shape=jax.ShapeDtypeStruct((m, width), F32),
        compiler_params=_params("parallel"),
    )(x, g.reshape(1, hd))


def _ffn_activation(g, u, ext_ref, w_ref, b_ref, tm):
    ext_ref[SUBLANES:SUBLANES + tm, :] = g
    w = w_ref[...]
    gc = w[0:1] * ext_ref[SUBLANES - 2:SUBLANES - 2 + tm, :]
    gc = gc + w[1:2] * ext_ref[SUBLANES - 1:SUBLANES - 1 + tm, :]
    gc = gc + w[2:3] * g
    gc = gc + b_ref[...]
    return gc * (1.0 / (1.0 + jnp.exp(-gc))) * u


def _ffn_fused_kernel(x_ref, wgu_ref, cw_ref, cb_ref, hist_ref, a_ref, st_ref, tail_ref, *, s, rc):
    tail_ref[SUBLANES - HIST:SUBLANES, :] = hist_ref[0]
    w = cw_ref[...]
    bias = cb_ref[...]
    tn = w.shape[1]
    row = lax.broadcasted_iota(jnp.int32, (SUBLANES, tn), 0)
    for c in range(s // rc):
        r0 = c * rc
        gu = jnp.dot(x_ref[0, r0:r0 + rc, :], wgu_ref[...], preferred_element_type=F32)
        g = gu[:, :tn]
        u = gu[:, tn:]
        p2 = tail_ref[SUBLANES - 2:SUBLANES - 1, :]
        p1 = tail_ref[SUBLANES - 1:SUBLANES, :]
        g1 = pltpu.roll(g, 1, axis=0)
        g2 = pltpu.roll(g, 2, axis=0)
        h1 = jnp.where(row == 0, p1, g1[0:SUBLANES])
        h2 = jnp.where(row == 0, p2, jnp.where(row == 1, p1, g2[0:SUBLANES]))
        g1 = jnp.concatenate([h1, g1[SUBLANES:]], axis=0)
        g2 = jnp.concatenate([h2, g2[SUBLANES:]], axis=0)
        tail_ref[...] = g[rc - SUBLANES:rc]
        gc = w[0:1] * g2 + w[1:2] * g1 + w[2:3] * g + bias
        a_ref[0, r0:r0 + rc, :] = (gc * (1.0 / (1.0 + jnp.exp(-gc))) * u).astype(a_ref.dtype)
    st_ref[0] = tail_ref[SUBLANES - HIST:SUBLANES, :]


def _ffn_fused(h3, wgu, cw, cb, hist, *, tn, rc=256):
    b, s, d = h3.shape
    f = wgu.shape[2] // 2
    rc = _pick(s, rc, SUBLANES)
    hspec = pl.BlockSpec((1, HIST, tn), lambda bi, j: (bi, 0, j))
    return pl.pallas_call(
        functools.partial(_ffn_fused_kernel, s=s, rc=rc),
        grid=(b, f // tn),
        in_specs=[pl.BlockSpec((1, s, d), lambda bi, j: (bi, 0, 0)),
                  pl.BlockSpec((None, d, 2 * tn), lambda bi, j: (0, 0, j)),
                  pl.BlockSpec((CONV_W, tn), lambda bi, j: (0, j)),
                  pl.BlockSpec((1, tn), lambda bi, j: (0, j)), hspec],
        out_specs=[pl.BlockSpec((1, s, tn), lambda bi, j: (bi, 0, j)), hspec],
        out_shape=[jax.ShapeDtypeStruct((b, s, f), BF16), jax.ShapeDtypeStruct((b, HIST, f), F32)],
        scratch_shapes=[pltpu.VMEM((SUBLANES, tn), F32)],
        compiler_params=_params("parallel", "parallel"),
    )(h3, wgu, cw, cb.reshape(1, f), hist)


def _ffn_act_kernel(g_ref, u_ref, cw_ref, cb_ref, hist_ref, a_ref, st_ref, ext_ref, *, tm):
    ext_ref[SUBLANES - HIST:SUBLANES, :] = hist_ref[0]
    a_ref[0] = _ffn_activation(g_ref[0], u_ref[0], ext_ref, cw_ref, cb_ref, tm).astype(a_ref.dtype)
    st_ref[0] = ext_ref[tm + SUBLANES - HIST:tm + SUBLANES, :]


def _ffn_act(g3, u3, cw, cb, hist, *, tn=1024):
    b, t, f = g3.shape
    tn = _pick(f, tn, LANES)
    blk = pl.BlockSpec((1, t, tn), lambda bi, j: (bi, 0, j))
    hspec = pl.BlockSpec((1, HIST, tn), lambda bi, j: (bi, 0, j))
    return pl.pallas_call(
        functools.partial(_ffn_act_kernel, tm=t),
        grid=(b, f // tn),
        in_specs=[blk, blk, pl.BlockSpec((CONV_W, tn), lambda bi, j: (0, j)),
                  pl.BlockSpec((1, tn), lambda bi, j: (0, j)), hspec],
        out_specs=[blk, hspec],
        out_shape=[jax.ShapeDtypeStruct((b, t, f), F32), jax.ShapeDtypeStruct((b, HIST, f), F32)],
        scratch_shapes=[pltpu.VMEM((t + SUBLANES, tn), F32)],
        compiler_params=_params("parallel", "parallel"),
    )(g3, u3, cw, cb.reshape(1, f), hist)


def kernel(x_prompt, x_sample, cache_k, cache_v, cache_mem_k, cache_mem_v, state_conv, state_ffn, page_table, mem_prompt, norm_mix, w_in, q_norm, k_norm, lambda_q1, lambda_k1, lambda_q2, lambda_k2, subln, sc_conv_w, mq_norm, norm_mem, w_mem_k, w_mem_v, mk_norm, w_out, norm_ffn, w_gate, w_up, ffn_conv_w, ffn_conv_b, w_down):
    b, s, d = x_prompt.shape
    bd, t, _ = x_sample.shape
    depth, n_pool, page, heads, _, dqk = cache_k.shape
    dv = cache_v.shape[-1]
    qcols = heads * 2 * dqk
    da_width = heads * dv
    sc_width = sc_conv_w.shape[-1]
    n_mem, mem_heads, mem_hd = cache_mem_k.shape[2:]
    mem_width = mem_heads * mem_hd
    d_ff = w_gate.shape[-1]
    n_pages = page_table.shape[1]
    past_len = n_pages * page
    c2 = 2 * qcols + da_width
    c5 = c2 + 3 * sc_width
    assert qcols == da_width and dv == 2 * dqk
    assert c2 % sc_width == 0 and c5 % mem_width == 0 and (2 * qcols) % dv == 0
    assert t <= page and (2 * heads * t) % SUBLANES == 0
    sc_block0, mq_block, v_block0 = c2 // sc_width, c5 // mem_width, (2 * qcols) // dv

    cos_p, sin_p = _rope_tables(jnp.arange(s, dtype=jnp.int32), dqk)
    cos_s, sin_s = _rope_tables(past_len + jnp.arange(t, dtype=jnp.int32), dqk)
    t_mq = 16
    assert t <= t_mq
    new_cols = -(-t * heads // LANES) * LANES
    ffn_tn = _pick(d_ff, 256, LANES)

    xp = x_prompt.reshape(b * s, d)
    xs = x_sample.reshape(bd * t, d)
    mem2 = mem_prompt.reshape(b * n_mem, d)
    zeros_conv = jnp.zeros((b, HIST, sc_width), F32)
    zeros_ffn = jnp.zeros((b, HIST, d_ff), F32)
    w_mem_k_b, w_mem_v_b = w_mem_k.astype(BF16), w_mem_v.astype(BF16)
    mem_k4 = cache_mem_k.reshape(depth, bd, n_mem, mem_width)
    mem_v4 = cache_mem_v.reshape(depth, bd, n_mem, mem_width)

    outs = {k: [] for k in ("kp", "vp", "cp", "fp", "mkp", "mvp", "ks", "vs", "cs", "fs")}
    for l in range(depth):
        lam_init = _lambda_init(l)
        lams = (lambda_q1[l], lambda_k1[l], lambda_q2[l], lambda_k2[l])

        proj, w_in_b = _matmul_cast(_rmsnorm(xs, norm_mix[l]), w_in, l)
        proj3 = proj.reshape(bd, t, -1)
        qb, kf, _ = _qk_rope(proj3, cos_s, sin_s, q_norm[l], k_norm[l], qcols=qcols, dqk=dqk, qscale=dqk ** -0.5)
        v_new = proj3[:, :, 2 * qcols:c2]
        qs = qb.reshape(bd, t, heads, 2, dqk).transpose(0, 3, 2, 1, 4).reshape(bd, 2, heads * t, dqk)
        kn = kf.reshape(bd, t, heads, 2, dqk).transpose(0, 3, 1, 2, 4).reshape(bd, 2, t * heads, dqk)
        kn = jnp.pad(kn, ((0, 0), (0, 0), (0, new_cols - t * heads), (0, 0)))
        vn = jnp.pad(v_new.reshape(bd, t * heads, dv), ((0, 0), (0, new_cols - t * heads), (0, 0)))
        da = _attn_paged(page_table, qs, cache_k, cache_v, kn, vn, lams, subln[l], layer=l, t_new=t,
                         lam_init=lam_init)
        da = da.reshape(bd, heads, t, dv).transpose(0, 2, 1, 3).reshape(bd, t, da_width)
        sc, conv_state = _sconv(proj3, sc_conv_w[l], state_conv[l], width=sc_width, block0=sc_block0, out_dtype=F32)
        mq = jnp.pad(proj3[:, :, c5:], ((0, 0), (0, t_mq - t), (0, 0)))
        mo = _mem_attn(mq, mem_k4, mem_v4, l, mq_norm[l], heads=mem_heads, hd=mem_hd, qblock=0,
                       out_dtype=F32)[:, :t]
        mix = jnp.concatenate([da, sc, mo], axis=-1).astype(BF16).reshape(bd * t, -1)
        xs, w_out_b = _matmul_cast(mix, w_out, l, xs)
        h2 = _rmsnorm(xs, norm_ffn[l])
        g2, u2, w_gu_b = _matmul_cast_pair(h2, w_gate, w_up, l, tn=ffn_tn)
        act, ffn_state = _ffn_act(g2.reshape(bd, t, d_ff), u2.reshape(bd, t, d_ff), ffn_conv_w[l], ffn_conv_b[l],
                                  state_ffn[l], tn=d_ff // 2)
        xs, w_down_b = _matmul_cast(act.astype(BF16).reshape(bd * t, d_ff), w_down, l, xs)

        outs["ks"].append(kf.reshape(bd, t, heads, 2, dqk))
        outs["vs"].append(v_new.reshape(bd, t, heads, dv))
        outs["cs"].append(conv_state)
        outs["fs"].append(ffn_state)

        hm = _rmsnorm(mem2, norm_mem[l])
        mk = _head_norm(_matmul(hm, w_mem_k_b, l), mk_norm[l], heads=mem_heads, hd=mem_hd)
        mv = _matmul(hm, w_mem_v_b, l)
        mk3 = mk.reshape(1, b, n_mem, mem_width)
        mv3 = mv.reshape(1, b, n_mem, mem_width)

        proj3 = _matmul(_rmsnorm(xp, norm_mix[l]), w_in_b, 0).reshape(b, s, -1)
        qb, kf, kb = _qk_rope(proj3, cos_p, sin_p, q_norm[l], k_norm[l], qcols=qcols, dqk=dqk,
                              qscale=dqk ** -0.5 * math.log2(math.e))
        da = _attn_prompt(qb, kb, proj3, lams, subln[l], heads=heads, dqk=dqk, dv=dv, vblock0=v_block0,
                          lam_init=lam_init)
        sc, conv_state = _sconv(proj3, sc_conv_w[l], zeros_conv, width=sc_width, block0=sc_block0, out_dtype=BF16)
        mo = _mem_attn(proj3, mk3, mv3, 0, mq_norm[l], heads=mem_heads, hd=mem_hd, qblock=mq_block, out_dtype=BF16)
        xp = _matmul([da.reshape(b * s, -1), sc.reshape(b * s, -1), mo.reshape(b * s, -1)], w_out_b, 0, xp)
        h2 = _rmsnorm(xp, norm_ffn[l]).reshape(b, s, d)
        act, ffn_state = _ffn_fused(h2, w_gu_b, ffn_conv_w[l], ffn_conv_b[l], zeros_ffn, tn=ffn_tn)
        xp = _matmul(act.reshape(b * s, d_ff), w_down_b, 0, xp, tm=512, tn=512)

        outs["kp"].append(kf.reshape(b, s, heads, 2, dqk))
        outs["vp"].append(proj3[:, :, 2 * qcols:c2].reshape(b, s, heads, dv))
        outs["cp"].append(conv_state)
        outs["fp"].append(ffn_state)
        outs["mkp"].append(mk.reshape(b, n_mem, mem_heads, mem_hd))
        outs["mvp"].append(mv.reshape(b, n_mem, mem_heads, mem_hd))

    st = lambda k: jnp.stack(outs[k])
    return (xp.reshape(b, s, d), xs.reshape(bd, t, d),
            st("kp"), st("vp"), st("cp"), st("fp"), st("mkp"), st("mvp"),
            st("ks"), st("vs"), st("cs"), st("fs"))
```

```python
import functools
import math

import jax
import jax.numpy as jnp
from jax import lax
from jax.experimental import pallas as pl
from jax.experimental.pallas import tpu as pltpu

ROPE_THETA = 10000.0
RMS_EPS = 1e-6
NEG_INF = -1e30
CONV_W = 3
HIST = CONV_W - 1
V7X_VMEM_BYTES = 64 * 1024 * 1024
VMEM_LIMIT = V7X_VMEM_BYTES - 8 * 1024 * 1024
SUBLANES = 8
LANES = 128

F32 = jnp.float32
BF16 = jnp.bfloat16


def _params(*sem):
    return pltpu.CompilerParams(dimension_semantics=sem, vmem_limit_bytes=VMEM_LIMIT)


def _pick(n, pref, align):
    if n <= pref:
        return n
    t = (pref // align) * align
    while t >= align:
        if n % t == 0:
            return t
        t -= align
    return n


def _lambda_init(l):
    return 0.8 - 0.6 * math.exp(-0.3 * l)


def _rmsnorm_kernel(x_ref, g_ref, o_ref):
    x = x_ref[...]
    ms = jnp.mean(x * x, axis=-1, keepdims=True)
    o_ref[...] = (x * lax.rsqrt(ms + RMS_EPS) * g_ref[...]).astype(o_ref.dtype)


def _rmsnorm(x, g):
    m, d = x.shape
    tm = _pick(m, 256, SUBLANES)
    return pl.pallas_call(
        _rmsnorm_kernel,
        grid=(m // tm,),
        in_specs=[pl.BlockSpec((tm, d), lambda i: (i, 0)),
                  pl.BlockSpec((1, d), lambda i: (0, 0))],
        out_specs=pl.BlockSpec((tm, d), lambda i: (i, 0)),
        out_shape=jax.ShapeDtypeStruct((m, d), BF16),
        compiler_params=_params("parallel"),
    )(x, g.reshape(1, d))


def _matmul_kernel(*refs, nparts, nk, has_res):
    xs, ws = refs[:nparts], refs[nparts:2 * nparts]
    r_ref = refs[2 * nparts] if has_res else None
    o_ref = refs[-1]
    d = None
    for x_ref, w_ref in zip(xs, ws):
        part = jnp.dot(x_ref[...], w_ref[...], preferred_element_type=F32)
        d = part if d is None else d + part
    if nk == 1:
        o_ref[...] = r_ref[...] + d if has_res else d
    else:
        k = pl.program_id(2)

        @pl.when(k == 0)
        def _():
            o_ref[...] = r_ref[...] + d if has_res else d

        @pl.when(k > 0)
        def _():
            o_ref[...] += d


def _matmul(xs, w, layer, res=None, *, tm=1024, tn=1024, tk=None):
    xs = list(xs) if isinstance(xs, (list, tuple)) else [xs]
    m = xs[0].shape[0]
    kdim, n = w.shape[1], w.shape[2]
    tm = _pick(m, tm, SUBLANES)
    tn = _pick(n, tn, LANES)
    nk = 1 if tk is None else kdim // tk
    assert nk == 1 or len(xs) == 1
    in_specs, w_specs, off = [], [], 0
    for x in xs:
        kp = x.shape[1] if nk == 1 else tk
        assert off % kp == 0
        in_specs.append(pl.BlockSpec((tm, kp), lambda i, j, k: (i, k)))
        w_specs.append(pl.BlockSpec((None, kp, tn), lambda i, j, k, r=off // kp: (layer, r + k, j)))
        off += x.shape[1]
    assert off == kdim
    in_specs += w_specs
    args = xs + [w] * len(xs)
    if res is not None:
        in_specs.append(pl.BlockSpec((tm, tn), lambda i, j, k: (i, j)))
        args.append(res)
    return pl.pallas_call(
        functools.partial(_matmul_kernel, nparts=len(xs), nk=nk, has_res=res is not None),
        grid=(m // tm, n // tn, nk),
        in_specs=in_specs,
        out_specs=pl.BlockSpec((tm, tn), lambda i, j, k: (i, j)),
        out_shape=jax.ShapeDtypeStruct((m, n), F32),
        compiler_params=_params("parallel", "parallel", "arbitrary"),
    )(*args)


def _matmul_cast_kernel(*refs, has_res):
    if has_res:
        x_ref, w_ref, r_ref, o_ref, wb_ref = refs
    else:
        x_ref, w_ref, o_ref, wb_ref = refs
    wb = w_ref[...].astype(BF16)
    wb_ref[...] = wb
    d = jnp.dot(x_ref[...], wb, preferred_element_type=F32)
    o_ref[...] = r_ref[...] + d if has_res else d


def _matmul_cast(x, w, layer, res=None):
    m, kdim = x.shape
    n = w.shape[2]
    tn = _pick(n, max(LANES, (8 * 1024 * 1024) // (4 * kdim) // LANES * LANES), LANES)
    in_specs = [pl.BlockSpec((m, kdim), lambda j: (0, 0)),
                pl.BlockSpec((None, kdim, tn), lambda j: (layer, 0, j))]
    args = [x, w]
    if res is not None:
        in_specs.append(pl.BlockSpec((m, tn), lambda j: (0, j)))
        args.append(res)
    return pl.pallas_call(
        functools.partial(_matmul_cast_kernel, has_res=res is not None),
        grid=(n // tn,),
        in_specs=in_specs,
        out_specs=[pl.BlockSpec((m, tn), lambda j: (0, j)), pl.BlockSpec((None, kdim, tn), lambda j: (0, 0, j))],
        out_shape=[jax.ShapeDtypeStruct((m, n), F32), jax.ShapeDtypeStruct((1, kdim, n), BF16)],
        compiler_params=_params("parallel"),
    )(*args)


def _matmul_cast_pair_kernel(x_ref, wa_ref, wb_ref, oa_ref, ob_ref, wab_ref, *, tn):
    wab = jnp.concatenate([wa_ref[...].astype(BF16), wb_ref[...].astype(BF16)], axis=1)
    wab_ref[...] = wab
    d = jnp.dot(x_ref[...], wab, preferred_element_type=F32)
    oa_ref[...] = d[:, :tn]
    ob_ref[...] = d[:, tn:]


def _matmul_cast_pair(x, wa, wb, layer, *, tn):
    m, kdim = x.shape
    n = wa.shape[2]
    wspec = pl.BlockSpec((None, kdim, tn), lambda j: (layer, 0, j))
    ospec = pl.BlockSpec((m, tn), lambda j: (0, j))
    return pl.pallas_call(
        functools.partial(_matmul_cast_pair_kernel, tn=tn),
        grid=(n // tn,),
        in_specs=[pl.BlockSpec((m, kdim), lambda j: (0, 0)), wspec, wspec],
        out_specs=[ospec, ospec, pl.BlockSpec((None, kdim, 2 * tn), lambda j: (0, 0, j))],
        out_shape=[jax.ShapeDtypeStruct((m, n), F32), jax.ShapeDtypeStruct((m, n), F32),
                   jax.ShapeDtypeStruct((1, kdim, 2 * n), BF16)],
        compiler_params=_params("parallel"),
    )(x, wa, wb)


def _qk_rope_kernel(q_ref, k_ref, cos_ref, sin_ref, qg_ref, kg_ref, qo_ref, ko_ref, kb_ref, *, nsub, dqk, qscale):
    c = cos_ref[...]
    s = sin_ref[...]

    def norm_rope(x, g):
        y = x * lax.rsqrt(jnp.mean(x * x, axis=-1, keepdims=True) + RMS_EPS) * g
        return y * c + pltpu.roll(y, dqk // 2, axis=1) * s

    for idx in range(nsub):
        sl = slice(idx * dqk, (idx + 1) * dqk)
        qo_ref[0, :, sl] = (norm_rope(q_ref[0, :, sl], qg_ref[...]) * qscale).astype(qo_ref.dtype)
        kr = norm_rope(k_ref[0, :, sl], kg_ref[...])
        ko_ref[0, :, sl] = kr
        kb_ref[0, :, sl] = kr.astype(kb_ref.dtype)


def _qk_rope(proj3, cos, sin, qg, kg, *, qcols, dqk, qscale):
    b, s, _ = proj3.shape
    tm = _pick(s, 256, SUBLANES)
    blk = lambda c: pl.BlockSpec((1, tm, qcols), lambda bi, i, c=c: (bi, i, c))
    tab = pl.BlockSpec((tm, dqk), lambda bi, i: (i, 0))
    gain = pl.BlockSpec((1, dqk), lambda bi, i: (0, 0))
    out = pl.BlockSpec((1, tm, qcols), lambda bi, i: (bi, i, 0))
    return pl.pallas_call(
        functools.partial(_qk_rope_kernel, nsub=qcols // dqk, dqk=dqk, qscale=qscale),
        grid=(b, s // tm),
        in_specs=[blk(0), blk(1), tab, tab, gain, gain],
        out_specs=[out, out, out],
        out_shape=[jax.ShapeDtypeStruct((b, s, qcols), BF16),
                   jax.ShapeDtypeStruct((b, s, qcols), F32),
                   jax.ShapeDtypeStruct((b, s, qcols), BF16)],
        compiler_params=_params("parallel", "parallel"),
    )(proj3, proj3, cos, sin, qg.reshape(1, dqk), kg.reshape(1, dqk))


def _rope_tables(pos, dqk):
    half = dqk // 2
    inv = ROPE_THETA ** (-jnp.arange(half, dtype=F32) / half)
    ang = pos.astype(F32)[:, None] * inv[None, :]
    cos, sin = jnp.cos(ang), jnp.sin(ang)
    return jnp.concatenate([cos, cos], axis=-1), jnp.concatenate([-sin, sin], axis=-1)


def _lambda_value(lq1, lk1, lq2, lk2, lam_init):
    a = jnp.sum(lq1[...] * lk1[...], axis=-1, keepdims=True)
    b = jnp.sum(lq2[...] * lk2[...], axis=-1, keepdims=True)
    return jnp.exp(a) - jnp.exp(b) + lam_init


def _diff_combine(o0, o1, lam, subln, lam_init):
    da = o0 - lam * o1
    y = da * lax.rsqrt(jnp.mean(da * da, axis=-1, keepdims=True) + RMS_EPS) * subln
    return y * (1.0 - lam_init)


def _attn_prompt_kernel(qi_ref, kj_ref, q_ref, k_ref, v_ref, lq1, lk1, lq2, lk2, sub_ref, o_ref, m_ref, l_ref, acc_ref,
                        *, t, rc, dqk, dv, lam_init):
    step = pl.program_id(2)
    i = qi_ref[step]
    j = kj_ref[step]

    @pl.when(j == 0)
    def _():
        m_ref[...] = jnp.full(m_ref.shape, NEG_INF, F32)
        l_ref[...] = jnp.zeros(l_ref.shape, F32)
        acc_ref[...] = jnp.zeros(acc_ref.shape, F32)

    def block(diagonal):
        vb = v_ref[0].astype(BF16)
        for c in range(t // rc):
            rows = slice(c * rc, (c + 1) * rc)
            kv = (c + 1) * rc if diagonal else t
            for mp in range(2):
                sl = slice(mp * dqk, (mp + 1) * dqk)
                s = lax.dot_general(q_ref[0, rows, sl], k_ref[0, 0:kv, sl], (((1,), (1,)), ((), ())),
                                    preferred_element_type=F32)
                if diagonal:
                    qpos = lax.broadcasted_iota(jnp.int32, (rc, kv), 0) + c * rc
                    kpos = lax.broadcasted_iota(jnp.int32, (rc, kv), 1)
                    s = jnp.where(kpos <= qpos, s, NEG_INF)
                m_prev = m_ref[mp, rows]
                m_new = jnp.maximum(m_prev, jnp.max(s, axis=-1, keepdims=True))
                corr = jnp.exp2(m_prev - m_new)
                p = jnp.exp2(s - pltpu.repeat(m_new, kv // LANES, axis=1))
                l_ref[mp, rows] = l_ref[mp, rows] * corr + jnp.sum(p, axis=-1, keepdims=True)
                acc_ref[mp, rows] = (acc_ref[mp, rows] * pltpu.repeat(corr, dv // LANES, axis=1)
                                     + jnp.dot(p.astype(BF16), vb[0:kv], preferred_element_type=F32))
                m_ref[mp, rows] = m_new

    @pl.when(j < i)
    def _():
        block(False)

    @pl.when(j == i)
    def _():
        block(True)
        lam = _lambda_value(lq1, lk1, lq2, lk2, lam_init)
        o0 = acc_ref[0] / pltpu.repeat(l_ref[0], dv // LANES, axis=1)
        o1 = acc_ref[1] / pltpu.repeat(l_ref[1], dv // LANES, axis=1)
        o_ref[0] = _diff_combine(o0, o1, lam, sub_ref[...], lam_init).astype(o_ref.dtype)


def _attn_prompt(qb, kb, proj3, lams, subln, *, heads, dqk, dv, vblock0, lam_init):
    b, s, _ = qb.shape
    t = _pick(s, 1024, LANES)
    n = s // t
    pairs =[(i, j) for i in range(n) for j in range(i + 1)]
    qi = jnp.asarray([p[0] for p in pairs], jnp.int32)
    kj = jnp.asarray([p[1] for p in pairs], jnp.int32)
    qspec = pl.BlockSpec((1, t, 2 * dqk), lambda bi, h, st, qi, kj: (bi, qi[st], h))
    kspec = pl.BlockSpec((1, t, 2 * dqk), lambda bi, h, st, qi, kj: (bi, kj[st], h))
    vspec = pl.BlockSpec((1, t, dv), lambda bi, h, st, qi, kj: (bi, kj[st], vblock0 + h))
    vec = lambda w: pl.BlockSpec((1, w), lambda bi, h, st, qi, kj: (0, 0))
    grid_spec = pltpu.PrefetchScalarGridSpec(
        num_scalar_prefetch=2,
        grid=(b, heads, len(pairs)),
        in_specs=[qspec, kspec, vspec, vec(dqk), vec(dqk), vec(dqk), vec(dqk), vec(dv)],
        out_specs=pl.BlockSpec((1, t, dv), lambda bi, h, st, qi, kj: (bi, qi[st], h)),
        scratch_shapes=[pltpu.VMEM((2, t, LANES), F32), pltpu.VMEM((2, t, LANES), F32), pltpu.VMEM((2, t, dv), F32)],
    )
    return pl.pallas_call(
        functools.partial(_attn_prompt_kernel, t=t, rc=min(t, 4 * LANES), dqk=dqk, dv=dv, lam_init=lam_init),
        grid_spec=grid_spec,
        out_shape=jax.ShapeDtypeStruct((b, s, heads * dv), BF16),
        compiler_params=_params("parallel", "parallel", "arbitrary"),
    )(qi, kj, qb, kb, proj3, *[x.reshape(1, dqk) for x in lams], subln.reshape(1, dv))


PAGES_PER_STEP = 8


def _attn_paged_kernel(*refs, t_new, heads, npp, lam_init):
    pt_ref, q_ref = refs[0], refs[1]
    k_refs, v_refs = refs[2:2 + npp], refs[2 + npp:2 + 2 * npp]
    kn_ref, vn_ref, lq1, lk1, lq2, lk2, sub_ref, o_ref, m_ref, l_ref, acc_ref = refs[2 + 2 * npp:]
    del pt_ref
    p = pl.program_id(1)
    hq = heads * t_new
    kcols = v_refs[0].shape[0]

    @pl.when(p == 0)
    def _():
        m_ref[...] = jnp.full(m_ref.shape, NEG_INF, F32)
        l_ref[...] = jnp.zeros(l_ref.shape, F32)
        acc_ref[...] = jnp.zeros(acc_ref.shape, F32)

    def scores(kmats, ncols):
        parts = [lax.dot_general(q_ref[0, mp], kmats[mp].astype(BF16), (((1,), (1,)), ((), ())),
                                 preferred_element_type=F32) for mp in range(2)]
        s = jnp.concatenate(parts, axis=0)
        qrow = lax.broadcasted_iota(jnp.int32, (2 * hq, ncols), 0) % hq
        col = lax.broadcasted_iota(jnp.int32, (2 * hq, ncols), 1)
        return s, qrow, col

    def update(s_list, v_list):
        m_prev = m_ref[...]
        m_new = m_prev
        for s in s_list:
            m_new = jnp.maximum(m_new, jnp.max(s, axis=-1, keepdims=True))
        corr = jnp.exp(m_prev - m_new)
        l_new = l_ref[...] * corr
        acc = acc_ref[...] * corr
        for s, v in zip(s_list, v_list):
            pm = jnp.exp(s - m_new)
            l_new = l_new + jnp.sum(pm, axis=-1, keepdims=True)
            acc = acc + jnp.dot(pm.astype(BF16), v.astype(BF16), preferred_element_type=F32)
        l_ref[...] = l_new
        acc_ref[...] = acc
        m_ref[...] = m_new

    s_list = []
    for r in range(npp):
        kmats = [k_refs[r][pl.ds(mp, kcols, stride=2), :] for mp in range(2)]
        s, qrow, col = scores(kmats, kcols)
        s_list.append(jnp.where(col % heads == qrow // t_new, s, NEG_INF))
    update(s_list, [v_refs[r][...] for r in range(npp)])

    @pl.when(p == pl.num_programs(1) - 1)
    def _():
        ncols = vn_ref.shape[1]
        s, qrow, col = scores([kn_ref[0, 0], kn_ref[0, 1]], ncols)
        keep = (col % heads == qrow // t_new) & (col // heads <= qrow % t_new)
        update([jnp.where(keep, s, NEG_INF)], [vn_ref[0]])
        lam = _lambda_value(lq1, lk1, lq2, lk2, lam_init)
        o = acc_ref[...] / l_ref[...]
        o_ref[0] = _diff_combine(o[0:hq], o[hq:2 * hq], lam, sub_ref[...], lam_init)


def _attn_paged(page_table, qs, cache_k, cache_v, k_new, v_new, lams, subln, *, layer, t_new, lam_init):
    bd, n_pages = page_table.shape
    depth, n_pool, page, heads, _, dqk = cache_k.shape
    dv = cache_v.shape[-1]
    npp = PAGES_PER_STEP if n_pages % PAGES_PER_STEP == 0 else 1
    hq = heads * t_new
    ck = cache_k.reshape(depth, n_pool, page * heads * 2, dqk)
    cv = cache_v.reshape(depth, n_pool, page * heads, dv)
    ncols = v_new.shape[1]

    def cspec(r, rows, width):
        return pl.BlockSpec((None, None, rows, width), lambda b, p, pt, r=r: (layer, pt[b, p * npp + r], 0, 0))

    vec = lambda w: pl.BlockSpec((1, w), lambda b, p, pt: (0, 0))
    grid_spec = pltpu.PrefetchScalarGridSpec(
        num_scalar_prefetch=1,
        grid=(bd, n_pages // npp),
        in_specs=[pl.BlockSpec((1, 2, hq, dqk), lambda b, p, pt: (b, 0, 0, 0))]
        + [cspec(r, page * heads * 2, dqk) for r in range(npp)]
        + [cspec(r, page * heads, dv) for r in range(npp)]
        + [pl.BlockSpec((1, 2, ncols, dqk), lambda b, p, pt: (b, 0, 0, 0)),
           pl.BlockSpec((1, ncols, dv), lambda b, p, pt: (b, 0, 0)),
           vec(dqk), vec(dqk), vec(dqk), vec(dqk), vec(dv)],
        out_specs=pl.BlockSpec((1, hq, dv), lambda b, p, pt: (b, 0, 0)),
        scratch_shapes=[pltpu.VMEM((2 * hq, 1), F32), pltpu.VMEM((2 * hq, 1), F32), pltpu.VMEM((2 * hq, dv), F32)],
    )
    return pl.pallas_call(
        functools.partial(_attn_paged_kernel, t_new=t_new, heads=heads, npp=npp, lam_init=lam_init),
        grid_spec=grid_spec,
        out_shape=jax.ShapeDtypeStruct((bd, hq, dv), F32),
        compiler_params=_params("parallel", "arbitrary"),
    )(page_table, qs, *([ck] * npp), *([cv] * npp), k_new, v_new,
      *[x.reshape(1, dqk) for x in lams], subln.reshape(1, dv))


def _sconv_kernel(gb_ref, gc_ref, hc_ref, w_ref, hist_ref, o_ref, st_ref, ext_ref, *, tm):
    i = pl.program_id(1)

    @pl.when(i == 0)
    def _():
        ext_ref[SUBLANES - HIST:SUBLANES, :] = hist_ref[0]

    u = gc_ref[0] * hc_ref[0]
    ext_ref[SUBLANES:SUBLANES + tm, :] = u
    w = w_ref[...]
    cu = w[0:1] * ext_ref[SUBLANES - 2:SUBLANES - 2 + tm, :]
    cu = cu + w[1:2] * ext_ref[SUBLANES - 1:SUBLANES - 1 + tm, :]
    cu = cu + w[2:3] * u
    o_ref[0] = (gb_ref[0] * cu).astype(o_ref.dtype)
    last = ext_ref[tm + SUBLANES - HIST:tm + SUBLANES, :]
    st_ref[0] = last
    ext_ref[SUBLANES - HIST:SUBLANES, :] = last


def _sconv(proj3, w, hist, *, width, block0, out_dtype):
    b, s, _ = proj3.shape
    tm = _pick(s, 512, SUBLANES)
    blk = lambda c: pl.BlockSpec((1, tm, width), lambda bi, i, c=c: (bi, i, block0 + c))
    return pl.pallas_call(
        functools.partial(_sconv_kernel, tm=tm),
        grid=(b, s // tm),
        in_specs=[blk(0), blk(1), blk(2),
                  pl.BlockSpec((CONV_W, width), lambda bi, i: (0, 0)),
                  pl.BlockSpec((1, HIST, width), lambda bi, i: (bi, 0, 0))],
        out_specs=[pl.BlockSpec((1, tm, width), lambda bi, i: (bi, i, 0)),
                   pl.BlockSpec((1, HIST, width), lambda bi, i: (bi, 0, 0))],
        out_shape=[jax.ShapeDtypeStruct((b, s, width), out_dtype),
                   jax.ShapeDtypeStruct((b, HIST, width), F32)],
        scratch_shapes=[pltpu.VMEM((tm + SUBLANES, width), F32)],
        compiler_params=_params("parallel", "arbitrary"),
    )(proj3, proj3, proj3, w, hist)


def _mem_attn_kernel(q_ref, k_ref, v_ref, g_ref, o_ref, *, heads, hd):
    scale = hd ** -0.5
    for h in range(heads):
        sl = slice(h * hd, (h + 1) * hd)
        q = q_ref[0, :, sl]
        qn = q * lax.rsqrt(jnp.mean(q * q, axis=-1, keepdims=True) + RMS_EPS) * g_ref[...]
        s = lax.dot_general((qn * scale).astype(BF16), k_ref[0, :, sl].astype(BF16), (((1,), (1,)), ((), ())),
                            preferred_element_type=F32)
        p = jnp.exp(s - jnp.max(s, axis=-1, keepdims=True))
        o = jnp.dot(p.astype(BF16), v_ref[0, :, sl].astype(BF16), preferred_element_type=F32)
        o_ref[0, :, sl] = (o / jnp.sum(p, axis=-1, keepdims=True)).astype(o_ref.dtype)


def _mem_attn(q3, mk, mv, layer, g, *, heads, hd, qblock, out_dtype):
    b, s, _ = q3.shape
    n = mk.shape[2]
    width = heads * hd
    tm = _pick(s, 512, SUBLANES)
    kv = pl.BlockSpec((None, 1, n, width), lambda bi, i: (layer, bi, 0, 0))
    return pl.pallas_call(
        functools.partial(_mem_attn_kernel, heads=heads, hd=hd),
        grid=(b, s // tm),
        in_specs=[pl.BlockSpec((1, tm, width), lambda bi, i: (bi, i, qblock)), kv, kv,
                  pl.BlockSpec((1, hd), lambda bi, i: (0, 0))],
        out_specs=pl.BlockSpec((1, tm, width), lambda bi, i: (bi, i, 0)),
        out_shape=jax.ShapeDtypeStruct((b, s, width), out_dtype),
        compiler_params=_params("parallel", "parallel"),
    )(q3, mk, mv, g.reshape(1, hd))


def _head_norm_kernel(x_ref, g_ref, o_ref, *, heads, hd):
    for h in range(heads):
        sl = slice(h * hd, (h + 1) * hd)
        x = x_ref[:, sl]
        o_ref[:, sl] = x * lax.rsqrt(jnp.mean(x * x, axis=-1, keepdims=True) + RMS_EPS) * g_ref[...]


def _head_norm(x, g, *, heads, hd):
    m, width = x.shape
    tm = _pick(m, 512, SUBLANES)
    return pl.pallas_call(
        functools.partial(_head_norm_kernel, heads=heads, hd=hd),
        grid=(m // tm,),
        in_specs=[pl.BlockSpec((tm, width), lambda i: (i, 0)), pl.BlockSpec((1, hd), lambda i: (0, 0))],
        out_specs=pl.BlockSpec((tm, width), lambda i: (i, 0)),
        out_shape=jax.ShapeDtypeStruct((m, width), F32),
        compiler_params=_params("parallel"),
    )(x, g.reshape(1, hd))


def _ffn_activation(g, u, ext_ref, w_ref, b_ref, tm):
    ext_ref[SUBLANES:SUBLANES + tm, :] = g
    w = w_ref[...]
    gc = w[0:1] * ext_ref[SUBLANES - 2:SUBLANES - 2 + tm, :]
    gc = gc + w[1:2] * ext_ref[SUBLANES - 1:SUBLANES - 1 + tm, :]
    gc = gc + w[2:3] * g
    gc = gc + b_ref[...]
    return gc * (1.0 / (1.0 + jnp.exp(-gc))) * u


def _ffn_fused_kernel(x_ref, wgu_ref, cw_ref, cb_ref, hist_ref, a_ref, st_ref, tail_ref, *, s, rc):
    tn = a_ref.shape[2]
    cols = pl.ds(pl.multiple_of(pl.program_id(1) * tn, tn), tn)
    tail_ref[SUBLANES - HIST:SUBLANES, :] = hist_ref[0, :, cols]
    w = cw_ref[:, cols]
    bias = cb_ref[:, cols]
    row = lax.broadcasted_iota(jnp.int32, (SUBLANES, tn), 0)
    for c in range(s // rc):
        r0 = c * rc
        gu = jnp.dot(x_ref[0, r0:r0 + rc, :], wgu_ref[...], preferred_element_type=F32)
        g = gu[:, :tn]
        u = gu[:, tn:]
        p2 = tail_ref[SUBLANES - 2:SUBLANES - 1, :]
        p1 = tail_ref[SUBLANES - 1:SUBLANES, :]
        g1 = pltpu.roll(g, 1, axis=0)
        g2 = pltpu.roll(g, 2, axis=0)
        h1 = jnp.where(row == 0, p1, g1[0:SUBLANES])
        h2 = jnp.where(row == 0, p2, jnp.where(row == 1, p1, g2[0:SUBLANES]))
        g1 = jnp.concatenate([h1, g1[SUBLANES:]], axis=0)
        g2 = jnp.concatenate([h2, g2[SUBLANES:]], axis=0)
        tail_ref[...] = g[rc - SUBLANES:rc]
        gc = w[0:1] * g2 + w[1:2] * g1 + w[2:3] * g + bias
        a_ref[0, r0:r0 + rc, :] = (gc * (1.0 / (1.0 + jnp.exp(-gc))) * u).astype(a_ref.dtype)
    st_ref[0] = tail_ref[SUBLANES - HIST:SUBLANES, :]


def _ffn_fused(h3, wgu, cw, cb, hist, *, tn, rc=256):
    b, s, d = h3.shape
    f = wgu.shape[2] // 2
    rc = _pick(s, rc, SUBLANES)
    hspec = pl.BlockSpec((1, HIST, tn), lambda bi, j: (bi, 0, j))
    return pl.pallas_call(
        functools.partial(_ffn_fused_kernel, s=s, rc=rc),
        grid=(b, f // tn),
        in_specs=[pl.BlockSpec((1, s, d), lambda bi, j: (bi, 0, 0)),
                  pl.BlockSpec((None, d, 2 * tn), lambda bi, j: (0, 0, j)),
                  pl.BlockSpec((CONV_W, f), lambda bi, j: (0, 0)),
                  pl.BlockSpec((1, f), lambda bi, j: (0, 0)),
                  pl.BlockSpec((1, HIST, f), lambda bi, j: (bi, 0, 0))],
        out_specs=[pl.BlockSpec((1, s, tn), lambda bi, j: (bi, 0, j)), hspec],
        out_shape=[jax.ShapeDtypeStruct((b, s, f), BF16), jax.ShapeDtypeStruct((b, HIST, f), F32)],
        scratch_shapes=[pltpu.VMEM((SUBLANES, tn), F32)],
        compiler_params=_params("parallel", "parallel"),
    )(h3, wgu, cw, cb.reshape(1, f), hist)


def _ffn_act_kernel(g_ref, u_ref, cw_ref, cb_ref, hist_ref, a_ref, st_ref, ext_ref, *, tm):
    ext_ref[SUBLANES - HIST:SUBLANES, :] = hist_ref[0]
    a_ref[0] = _ffn_activation(g_ref[0], u_ref[0], ext_ref, cw_ref, cb_ref, tm).astype(a_ref.dtype)
    st_ref[0] = ext_ref[tm + SUBLANES - HIST:tm + SUBLANES, :]


def _ffn_act(g3, u3, cw, cb, hist, *, tn=1024):
    b, t, f = g3.shape
    tn = _pick(f, tn, LANES)
    blk = pl.BlockSpec((1, t, tn), lambda bi, j: (bi, 0, j))
    hspec = pl.BlockSpec((1, HIST, tn), lambda bi, j: (bi, 0, j))
    return pl.pallas_call(
        functools.partial(_ffn_act_kernel, tm=t),
        grid=(b, f // tn),
        in_specs=[blk, blk, pl.BlockSpec((CONV_W, tn), lambda bi, j: (0, j)),
                  pl.BlockSpec((1, tn), lambda bi, j: (0, j)), hspec],
        out_specs=[blk, hspec],
        out_shape=[jax.ShapeDtypeStruct((b, t, f), F32), jax.ShapeDtypeStruct((b, HIST, f), F32)],
        scratch_shapes=[pltpu.VMEM((t + SUBLANES, tn), F32)],
        compiler_params=_params("parallel", "parallel"),
    )(g3, u3, cw, cb.reshape(1, f), hist)


def kernel(x_prompt, x_sample, cache_k, cache_v, cache_mem_k, cache_mem_v, state_conv, state_ffn, page_table, mem_prompt, norm_mix, w_in, q_norm, k_norm, lambda_q1, lambda_k1, lambda_q2, lambda_k2, subln, sc_conv_w, mq_norm, norm_mem, w_mem_k, w_mem_v, mk_norm, w_out, norm_ffn, w_gate, w_up, ffn_conv_w, ffn_conv_b, w_down):
    b, s, d = x_prompt.shape
    bd, t, _ = x_sample.shape
    depth, n_pool, page, heads, _, dqk = cache_k.shape
    dv = cache_v.shape[-1]
    qcols = heads * 2 * dqk
    da_width = heads * dv
    sc_width = sc_conv_w.shape[-1]
    n_mem, mem_heads, mem_hd = cache_mem_k.shape[2:]
    mem_width = mem_heads * mem_hd
    d_ff = w_gate.shape[-1]
    n_pages = page_table.shape[1]
    past_len = n_pages * page
    c2 = 2 * qcols + da_width
    c5 = c2 + 3 * sc_width
    assert qcols == da_width and dv == 2 * dqk
    assert c2 % sc_width == 0 and c5 % mem_width == 0 and (2 * qcols) % dv == 0
    assert t <= page and (2 * heads * t) % SUBLANES == 0
    sc_block0, mq_block, v_block0 = c2 // sc_width, c5 // mem_width, (2 * qcols) // dv

    cos_p, sin_p = _rope_tables(jnp.arange(s, dtype=jnp.int32), dqk)
    cos_s, sin_s = _rope_tables(past_len + jnp.arange(t, dtype=jnp.int32), dqk)
    t_mq = 16
    assert t <= t_mq
    new_cols = -(-t * heads // LANES) * LANES
    ffn_tn = _pick(d_ff, 256, LANES)

    xp = x_prompt.reshape(b * s, d)
    xs = x_sample.reshape(bd * t, d)
    mem2 = mem_prompt.reshape(b * n_mem, d)
    zeros_conv = jnp.zeros((b, HIST, sc_width), F32)
    zeros_ffn = jnp.zeros((b, HIST, d_ff), F32)
    w_mem_k_b, w_mem_v_b = w_mem_k.astype(BF16), w_mem_v.astype(BF16)
    mem_k4 = cache_mem_k.reshape(depth, bd, n_mem, mem_width)
    mem_v4 = cache_mem_v.reshape(depth, bd, n_mem, mem_width)

    outs = {k: [] for k in ("kp", "vp", "cp", "fp", "mkp", "mvp", "ks", "vs", "cs", "fs")}
    for l in range(depth):
        lam_init = _lambda_init(l)
        lams = (lambda_q1[l], lambda_k1[l], lambda_q2[l], lambda_k2[l])

        proj, w_in_b = _matmul_cast(_rmsnorm(xs, norm_mix[l]), w_in, l)
        proj3 = proj.reshape(bd, t, -1)
        qb, kf, _ = _qk_rope(proj3, cos_s, sin_s, q_norm[l], k_norm[l], qcols=qcols, dqk=dqk, qscale=dqk ** -0.5)
        v_new = proj3[:, :, 2 * qcols:c2]
        qs = qb.reshape(bd, t, heads, 2, dqk).transpose(0, 3, 2, 1, 4).reshape(bd, 2, heads * t, dqk)
        kn = kf.reshape(bd, t, heads, 2, dqk).transpose(0, 3, 1, 2, 4).reshape(bd, 2, t * heads, dqk)
        kn = jnp.pad(kn, ((0, 0), (0, 0), (0, new_cols - t * heads), (0, 0)))
        vn = jnp.pad(v_new.reshape(bd, t * heads, dv), ((0, 0), (0, new_cols - t * heads), (0, 0)))
        da = _attn_paged(page_table, qs, cache_k, cache_v, kn, vn, lams, subln[l], layer=l, t_new=t,
                         lam_init=lam_init)
        da = da.reshape(bd, heads, t, dv).transpose(0, 2, 1, 3).reshape(bd, t, da_width)
        sc, conv_state = _sconv(proj3, sc_conv_w[l], state_conv[l], width=sc_width, block0=sc_block0, out_dtype=F32)
        mq = jnp.pad(proj3[:, :, c5:], ((0, 0), (0, t_mq - t), (0, 0)))
        mo = _mem_attn(mq, mem_k4, mem_v4, l, mq_norm[l], heads=mem_heads, hd=mem_hd, qblock=0,
                       out_dtype=F32)[:, :t]
        mix = jnp.concatenate([da, sc, mo], axis=-1).astype(BF16).reshape(bd * t, -1)
        xs, w_out_b = _matmul_cast(mix, w_out, l, xs)
        h2 = _rmsnorm(xs, norm_ffn[l])
        g2, u2, w_gu_b = _matmul_cast_pair(h2, w_gate, w_up, l, tn=ffn_tn)
        act, ffn_state = _ffn_act(g2.reshape(bd, t, d_ff), u2.reshape(bd, t, d_ff), ffn_conv_w[l], ffn_conv_b[l],
                                  state_ffn[l], tn=d_ff // 2)
        xs, w_down_b = _matmul_cast(act.astype(BF16).reshape(bd * t, d_ff), w_down, l, xs)

        outs["ks"].append(kf.reshape(bd, t, heads, 2, dqk))
        outs["vs"].append(v_new.reshape(bd, t, heads, dv))
        outs["cs"].append(conv_state)
        outs["fs"].append(ffn_state)

        hm = _rmsnorm(mem2, norm_mem[l])
        mk = _head_norm(_matmul(hm, w_mem_k_b, l), mk_norm[l], heads=mem_heads, hd=mem_hd)
        mv = _matmul(hm, w_mem_v_b, l)
        mk3 = mk.reshape(1, b, n_mem, mem_width)
        mv3 = mv.reshape(1, b, n_mem, mem_width)

        proj3 = _matmul(_rmsnorm(xp, norm_mix[l]), w_in_b, 0).reshape(b, s, -1)
        qb, kf, kb = _qk_rope(proj3, cos_p, sin_p, q_norm[l], k_norm[l], qcols=qcols, dqk=dqk,
                              qscale=dqk ** -0.5 * math.log2(math.e))
        da = _attn_prompt(qb, kb, proj3, lams, subln[l], heads=heads, dqk=dqk, dv=dv, vblock0=v_block0,
                          lam_init=lam_init)
        sc, conv_state = _sconv(proj3, sc_conv_w[l], zeros_conv, width=sc_width, block0=sc_block0, out_dtype=BF16)
        mo = _mem_attn(proj3, mk3, mv3, 0, mq_norm[l], heads=mem_heads, hd=mem_hd, qblock=mq_block, out_dtype=BF16)
        xp = _matmul([da.reshape(b * s, -1), sc.reshape(b * s, -1), mo.reshape(b * s, -1)], w_out_b, 0, xp)
        h2 = _rmsnorm(xp, norm_ffn[l]).reshape(b, s, d)
        act, ffn_state = _ffn_fused(h2, w_gu_b, ffn_conv_w[l], ffn_conv_b[l], zeros_ffn, tn=ffn_tn)
        xp = _matmul(act.reshape(b * s, d_ff), w_down_b, 0, xp, tm=512, tn=512)

        outs["kp"].append(kf.reshape(b, s, heads, 2, dqk))
        outs["vp"].append(proj3[:, :, 2 * qcols:c2].reshape(b, s, heads, dv))
        outs["cp"].append(conv_state)
        outs["fp"].append(ffn_state)
        outs["mkp"].append(mk.reshape(b, n_mem, mem_heads, mem_hd))
        outs["mvp"].append(mv.reshape(b, n_mem, mem_heads, mem_hd))

    st = lambda k: jnp.stack(outs[k])
    return (xp.reshape(b, s, d), xs.reshape(bd, t, d),
            st("kp"), st("vp"), st("cp"), st("fp"), st("mkp"), st("mvp"),
            st("ks"), st("vs"), st("cs"), st("fs"))
```

```python
import functools
import math

import jax
import jax.numpy as jnp
from jax import lax
from jax.experimental import pallas as pl
from jax.experimental.pallas import tpu as pltpu

ROPE_THETA = 10000.0
RMS_EPS = 1e-6
NEG_INF = -1e30
CONV_W = 3
HIST = CONV_W - 1
V7X_VMEM_BYTES = 64 * 1024 * 1024
VMEM_LIMIT = V7X_VMEM_BYTES - 8 * 1024 * 1024
SUBLANES = 8
LANES = 128

F32 = jnp.float32
BF16 = jnp.bfloat16


def _params(*sem):
    return pltpu.CompilerParams(dimension_semantics=sem, vmem_limit_bytes=VMEM_LIMIT)


def _pick(n, pref, align):
    if n <= pref:
        return n
    t = (pref // align) * align
    while t >= align:
        if n % t == 0:
            return t
        t -= align
    return n


def _lambda_init(l):
    return 0.8 - 0.6 * math.exp(-0.3 * l)


def _rmsnorm_kernel(x_ref, g_ref, o_ref):
    x = x_ref[...]
    ms = jnp.mean(x * x, axis=-1, keepdims=True)
    o_ref[...] = (x * lax.rsqrt(ms + RMS_EPS) * g_ref[...]).astype(o_ref.dtype)


def _rmsnorm(x, g):
    m, d = x.shape
    tm = _pick(m, 256, SUBLANES)
    return pl.pallas_call(
        _rmsnorm_kernel,
        grid=(m // tm,),
        in_specs=[pl.BlockSpec((tm, d), lambda i: (i, 0)),
                  pl.BlockSpec((1, d), lambda i: (0, 0))],
        out_specs=pl.BlockSpec((tm, d), lambda i: (i, 0)),
        out_shape=jax.ShapeDtypeStruct((m, d), BF16),
        compiler_params=_params("parallel"),
    )(x, g.reshape(1, d))


def _matmul_kernel(*refs, nparts, nk, has_res):
    xs, ws = refs[:nparts], refs[nparts:2 * nparts]
    r_ref = refs[2 * nparts] if has_res else None
    o_ref = refs[-1]
    d = None
    for x_ref, w_ref in zip(xs, ws):
        part = jnp.dot(x_ref[...], w_ref[...], preferred_element_type=F32)
        d = part if d is None else d + part
    if nk == 1:
        o_ref[...] = r_ref[...] + d if has_res else d
    else:
        k = pl.program_id(2)

        @pl.when(k == 0)
        def _():
            o_ref[...] = r_ref[...] + d if has_res else d

        @pl.when(k > 0)
        def _():
            o_ref[...] += d


def _matmul(xs, w, layer, res=None, *, tm=1024, tn=1024, tk=None):
    xs = list(xs) if isinstance(xs, (list, tuple)) else [xs]
    m = xs[0].shape[0]
    kdim, n = w.shape[1], w.shape[2]
    tm = _pick(m, tm, SUBLANES)
    tn = _pick(n, tn, LANES)
    nk = 1 if tk is None else kdim // tk
    assert nk == 1 or len(xs) == 1
    in_specs, w_specs, off = [], [], 0
    for x in xs:
        kp = x.shape[1] if nk == 1 else tk
        assert off % kp == 0
        in_specs.append(pl.BlockSpec((tm, kp), lambda i, j, k: (i, k)))
        w_specs.append(pl.BlockSpec((None, kp, tn), lambda i, j, k, r=off // kp: (layer, r + k, j)))
        off += x.shape[1]
    assert off == kdim
    in_specs += w_specs
    args = xs + [w] * len(xs)
    if res is not None:
        in_specs.append(pl.BlockSpec((tm, tn), lambda i, j, k: (i, j)))
        args.append(res)
    return pl.pallas_call(
        functools.partial(_matmul_kernel, nparts=len(xs), nk=nk, has_res=res is not None),
        grid=(m // tm, n // tn, nk),
        in_specs=in_specs,
        out_specs=pl.BlockSpec((tm, tn), lambda i, j, k: (i, j)),
        out_shape=jax.ShapeDtypeStruct((m, n), F32),
        compiler_params=_params("parallel", "parallel", "arbitrary"),
    )(*args)


def _matmul_cast_kernel(*refs, has_res):
    if has_res:
        x_ref, w_ref, r_ref, o_ref, wb_ref = refs
    else:
        x_ref, w_ref, o_ref, wb_ref = refs
    wb = w_ref[...].astype(BF16)
    wb_ref[...] = wb
    d = jnp.dot(x_ref[...], wb, preferred_element_type=F32)
    o_ref[...] = r_ref[...] + d if has_res else d


def _matmul_cast(x, w, layer, res=None):
    m, kdim = x.shape
    n = w.shape[2]
    tn = _pick(n, max(LANES, (8 * 1024 * 1024) // (4 * kdim) // LANES * LANES), LANES)
    in_specs = [pl.BlockSpec((m, kdim), lambda j: (0, 0)),
                pl.BlockSpec((None, kdim, tn), lambda j: (layer, 0, j))]
    args = [x, w]
    if res is not None:
        in_specs.append(pl.BlockSpec((m, tn), lambda j: (0, j)))
        args.append(res)
    return pl.pallas_call(
        functools.partial(_matmul_cast_kernel, has_res=res is not None),
        grid=(n // tn,),
        in_specs=in_specs,
        out_specs=[pl.BlockSpec((m, tn), lambda j: (0, j)), pl.BlockSpec((None, kdim, tn), lambda j: (0, 0, j))],
        out_shape=[jax.ShapeDtypeStruct((m, n), F32), jax.ShapeDtypeStruct((1, kdim, n), BF16)],
        compiler_params=_params("parallel"),
    )(*args)


def _matmul_cast_pair_kernel(x_ref, wa_ref, wb_ref, oa_ref, ob_ref, wab_ref, *, tn):
    wab = jnp.concatenate([wa_ref[...].astype(BF16), wb_ref[...].astype(BF16)], axis=1)
    wab_ref[...] = wab
    d = jnp.dot(x_ref[...], wab, preferred_element_type=F32)
    oa_ref[...] = d[:, :tn]
    ob_ref[...] = d[:, tn:]


def _matmul_cast_pair(x, wa, wb, layer, *, tn):
    m, kdim = x.shape
    n = wa.shape[2]
    wspec = pl.BlockSpec((None, kdim, tn), lambda j: (layer, 0, j))
    ospec = pl.BlockSpec((m, tn), lambda j: (0, j))
    return pl.pallas_call(
        functools.partial(_matmul_cast_pair_kernel, tn=tn),
        grid=(n // tn,),
        in_specs=[pl.BlockSpec((m, kdim), lambda j: (0, 0)), wspec, wspec],
        out_specs=[ospec, ospec, pl.BlockSpec((None, kdim, 2 * tn), lambda j: (0, 0, j))],
        out_shape=[jax.ShapeDtypeStruct((m, n), F32), jax.ShapeDtypeStruct((m, n), F32),
                   jax.ShapeDtypeStruct((1, kdim, 2 * n), BF16)],
        compiler_params=_params("parallel"),
    )(x, wa, wb)


def _qk_rope_kernel(q_ref, k_ref, cos_ref, sin_ref, qg_ref, kg_ref, qo_ref, ko_ref, kb_ref, *, nsub, dqk, qscale):
    c = cos_ref[...]
    s = sin_ref[...]

    def norm_rope(x, g):
        y = x * lax.rsqrt(jnp.mean(x * x, axis=-1, keepdims=True) + RMS_EPS) * g
        return y * c + pltpu.roll(y, dqk // 2, axis=1) * s

    for idx in range(nsub):
        sl = slice(idx * dqk, (idx + 1) * dqk)
        qo_ref[0, :, sl] = (norm_rope(q_ref[0, :, sl], qg_ref[...]) * qscale).astype(qo_ref.dtype)
        kr = norm_rope(k_ref[0, :, sl], kg_ref[...])
        ko_ref[0, :, idx // 2, idx % 2, :] = kr
        kb_ref[0, :, sl] = kr.astype(kb_ref.dtype)


def _qk_rope(proj3, cos, sin, qg, kg, *, qcols, dqk, qscale):
    b, s, _ = proj3.shape
    tm = _pick(s, 256, SUBLANES)
    nh = qcols // (2 * dqk)
    blk = lambda c: pl.BlockSpec((1, tm, qcols), lambda bi, i, c=c: (bi, i, c))
    tab = pl.BlockSpec((tm, dqk), lambda bi, i: (i, 0))
    gain = pl.BlockSpec((1, dqk), lambda bi, i: (0, 0))
    out = pl.BlockSpec((1, tm, qcols), lambda bi, i: (bi, i, 0))
    out5 = pl.BlockSpec((1, tm, nh, 2, dqk), lambda bi, i: (bi, i, 0, 0, 0))
    return pl.pallas_call(
        functools.partial(_qk_rope_kernel, nsub=qcols // dqk, dqk=dqk, qscale=qscale),
        grid=(b, s // tm),
        in_specs=[blk(0), blk(1), tab, tab, gain, gain],
        out_specs=[out, out5, out],
        out_shape=[jax.ShapeDtypeStruct((b, s, qcols), BF16),
                   jax.ShapeDtypeStruct((b, s, nh, 2, dqk), F32),
                   jax.ShapeDtypeStruct((b, s, qcols), BF16)],
        compiler_params=_params("parallel", "parallel"),
    )(proj3, proj3, cos, sin, qg.reshape(1, dqk), kg.reshape(1, dqk))


def _rope_tables(pos, dqk):
    half = dqk // 2
    inv = ROPE_THETA ** (-jnp.arange(half, dtype=F32) / half)
    ang = pos.astype(F32)[:, None] * inv[None, :]
    cos, sin = jnp.cos(ang), jnp.sin(ang)
    return jnp.concatenate([cos, cos], axis=-1), jnp.concatenate([-sin, sin], axis=-1)


def _lambda_value(lq1, lk1, lq2, lk2, lam_init):
    a = jnp.sum(lq1[...] * lk1[...], axis=-1, keepdims=True)
    b = jnp.sum(lq2[...] * lk2[...], axis=-1, keepdims=True)
    return jnp.exp(a) - jnp.exp(b) + lam_init


def _diff_combine(o0, o1, lam, subln, lam_init):
    da = o0 - lam * o1
    y = da * lax.rsqrt(jnp.mean(da * da, axis=-1, keepdims=True) + RMS_EPS) * subln
    return y * (1.0 - lam_init)


def _attn_prompt_kernel(qi_ref, kj_ref, q_ref, k_ref, v_ref, lq1, lk1, lq2, lk2, sub_ref, o_ref, m_ref, l_ref, acc_ref,
                        *, t, rc, dqk, dv, lam_init):
    step = pl.program_id(2)
    i = qi_ref[step]
    j = kj_ref[step]

    @pl.when(j == 0)
    def _():
        m_ref[...] = jnp.full(m_ref.shape, NEG_INF, F32)
        l_ref[...] = jnp.zeros(l_ref.shape, F32)
        acc_ref[...] = jnp.zeros(acc_ref.shape, F32)

    def block(diagonal):
        vb = v_ref[0].astype(BF16)
        for c in range(t // rc):
            rows = slice(c * rc, (c + 1) * rc)
            kv = (c + 1) * rc if diagonal else t
            for mp in range(2):
                sl = slice(mp * dqk, (mp + 1) * dqk)
                s = lax.dot_general(q_ref[0, rows, sl], k_ref[0, 0:kv, sl], (((1,), (1,)), ((), ())),
                                    preferred_element_type=F32)
                if diagonal:
                    qpos = lax.broadcasted_iota(jnp.int32, (rc, kv), 0) + c * rc
                    kpos = lax.broadcasted_iota(jnp.int32, (rc, kv), 1)
                    s = jnp.where(kpos <= qpos, s, NEG_INF)
                m_prev = m_ref[mp, rows]
                m_new = jnp.maximum(m_prev, jnp.max(s, axis=-1, keepdims=True))
                corr = jnp.exp2(m_prev - m_new)
                p = jnp.exp2(s - pltpu.repeat(m_new, kv // LANES, axis=1))
                l_ref[mp, rows] = l_ref[mp, rows] * corr + jnp.sum(p, axis=-1, keepdims=True)
                acc_ref[mp, rows] = (acc_ref[mp, rows] * pltpu.repeat(corr, dv // LANES, axis=1)
                                     + jnp.dot(p.astype(BF16), vb[0:kv], preferred_element_type=F32))
                m_ref[mp, rows] = m_new

    @pl.when(j < i)
    def _():
        block(False)

    @pl.when(j == i)
    def _():
        block(True)
        lam = _lambda_value(lq1, lk1, lq2, lk2, lam_init)
        o0 = acc_ref[0] / pltpu.repeat(l_ref[0], dv // LANES, axis=1)
        o1 = acc_ref[1] / pltpu.repeat(l_ref[1], dv // LANES, axis=1)
        o_ref[0] = _diff_combine(o0, o1, lam, sub_ref[...], lam_init).astype(o_ref.dtype)


def _attn_prompt(qb, kb, proj3, lams, subln, *, heads, dqk, dv, vblock0, lam_init):
    b, s, _ = qb.shape
    t = _pick(s, 1024, LANES)
    n = s // t
    pairs =[(i, j) for i in range(n) for j in range(i + 1)]
    qi = jnp.asarray([p[0] for p in pairs], jnp.int32)
    kj = jnp.asarray([p[1] for p in pairs], jnp.int32)
    qspec = pl.BlockSpec((1, t, 2 * dqk), lambda bi, h, st, qi, kj: (bi, qi[st], h))
    kspec = pl.BlockSpec((1, t, 2 * dqk), lambda bi, h, st, qi, kj: (bi, kj[st], h))
    vspec = pl.BlockSpec((1, t, dv), lambda bi, h, st, qi, kj: (bi, kj[st], vblock0 + h))
    vec = lambda w: pl.BlockSpec((1, w), lambda bi, h, st, qi, kj: (0, 0))
    grid_spec = pltpu.PrefetchScalarGridSpec(
        num_scalar_prefetch=2,
        grid=(b, heads, len(pairs)),
        in_specs=[qspec, kspec, vspec, vec(dqk), vec(dqk), vec(dqk), vec(dqk), vec(dv)],
        out_specs=pl.BlockSpec((1, t, dv), lambda bi, h, st, qi, kj: (bi, qi[st], h)),
        scratch_shapes=[pltpu.VMEM((2, t, LANES), F32), pltpu.VMEM((2, t, LANES), F32), pltpu.VMEM((2, t, dv), F32)],
    )
    return pl.pallas_call(
        functools.partial(_attn_prompt_kernel, t=t, rc=min(t, 4 * LANES), dqk=dqk, dv=dv, lam_init=lam_init),
        grid_spec=grid_spec,
        out_shape=jax.ShapeDtypeStruct((b, s, heads * dv), BF16),
        compiler_params=_params("parallel", "parallel", "arbitrary"),
    )(qi, kj, qb, kb, proj3, *[x.reshape(1, dqk) for x in lams], subln.reshape(1, dv))


PAGES_PER_STEP = 8


def _attn_paged_kernel(*refs, t_new, heads, npp, lam_init):
    pt_ref, q_ref = refs[0], refs[1]
    k_refs, v_refs = refs[2:2 + npp], refs[2 + npp:2 + 2 * npp]
    kn_ref, vn_ref, lq1, lk1, lq2, lk2, sub_ref, o_ref, m_ref, l_ref, acc_ref = refs[2 + 2 * npp:]
    del pt_ref
    p = pl.program_id(1)
    hq = heads * t_new
    kcols = v_refs[0].shape[0]

    @pl.when(p == 0)
    def _():
        m_ref[...] = jnp.full(m_ref.shape, NEG_INF, F32)
        l_ref[...] = jnp.zeros(l_ref.shape, F32)
        acc_ref[...] = jnp.zeros(acc_ref.shape, F32)

    def scores(kmats, ncols):
        parts = [lax.dot_general(q_ref[0, mp], kmats[mp].astype(BF16), (((1,), (1,)), ((), ())),
                                 preferred_element_type=F32) for mp in range(2)]
        s = jnp.concatenate(parts, axis=0)
        qrow = lax.broadcasted_iota(jnp.int32, (2 * hq, ncols), 0) % hq
        col = lax.broadcasted_iota(jnp.int32, (2 * hq, ncols), 1)
        return s, qrow, col

    def update(s_list, v_list):
        m_prev = m_ref[...]
        m_new = m_prev
        for s in s_list:
            m_new = jnp.maximum(m_new, jnp.max(s, axis=-1, keepdims=True))
        corr = jnp.exp(m_prev - m_new)
        l_new = l_ref[...] * corr
        acc = acc_ref[...] * corr
        for s, v in zip(s_list, v_list):
            pm = jnp.exp(s - m_new)
            l_new = l_new + jnp.sum(pm, axis=-1, keepdims=True)
            acc = acc + jnp.dot(pm.astype(BF16), v.astype(BF16), preferred_element_type=F32)
        l_ref[...] = l_new
        acc_ref[...] = acc
        m_ref[...] = m_new

    s_list = []
    for r in range(npp):
        kmats = [k_refs[r][pl.ds(mp, kcols, stride=2), :] for mp in range(2)]
        s, qrow, col = scores(kmats, kcols)
        s_list.append(jnp.where(col % heads == qrow // t_new, s, NEG_INF))
    update(s_list, [v_refs[r][...] for r in range(npp)])

    @pl.when(p == pl.num_programs(1) - 1)
    def _():
        ncols = vn_ref.shape[1]
        s, qrow, col = scores([kn_ref[0, 0], kn_ref[0, 1]], ncols)
        keep = (col % heads == qrow // t_new) & (col // heads <= qrow % t_new)
        update([jnp.where(keep, s, NEG_INF)], [vn_ref[0]])
        lam = _lambda_value(lq1, lk1, lq2, lk2, lam_init)
        o = acc_ref[...] / l_ref[...]
        o_ref[0] = _diff_combine(o[0:hq], o[hq:2 * hq], lam, sub_ref[...], lam_init)


def _attn_paged(page_table, qs, cache_k, cache_v, k_new, v_new, lams, subln, *, layer, t_new, lam_init):
    bd, n_pages = page_table.shape
    depth, n_pool, page, heads, _, dqk = cache_k.shape
    dv = cache_v.shape[-1]
    npp = PAGES_PER_STEP if n_pages % PAGES_PER_STEP == 0 else 1
    hq = heads * t_new
    ck = cache_k.reshape(depth, n_pool, page * heads * 2, dqk)
    cv = cache_v.reshape(depth, n_pool, page * heads, dv)
    ncols = v_new.shape[1]

    def cspec(r, rows, width):
        return pl.BlockSpec((None, None, rows, width), lambda b, p, pt, r=r: (layer, pt[b, p * npp + r], 0, 0))

    vec = lambda w: pl.BlockSpec((1, w), lambda b, p, pt: (0, 0))
    grid_spec = pltpu.PrefetchScalarGridSpec(
        num_scalar_prefetch=1,
        grid=(bd, n_pages // npp),
        in_specs=[pl.BlockSpec((1, 2, hq, dqk), lambda b, p, pt: (b, 0, 0, 0))]
        + [cspec(r, page * heads * 2, dqk) for r in range(npp)]
        + [cspec(r, page * heads, dv) for r in range(npp)]
        + [pl.BlockSpec((1, 2, ncols, dqk), lambda b, p, pt: (b, 0, 0, 0)),
           pl.BlockSpec((1, ncols, dv), lambda b, p, pt: (b, 0, 0)),
           vec(dqk), vec(dqk), vec(dqk), vec(dqk), vec(dv)],
        out_specs=pl.BlockSpec((1, hq, dv), lambda b, p, pt: (b, 0, 0)),
        scratch_shapes=[pltpu.VMEM((2 * hq, 1), F32), pltpu.VMEM((2 * hq, 1), F32), pltpu.VMEM((2 * hq, dv), F32)],
    )
    return pl.pallas_call(
        functools.partial(_attn_paged_kernel, t_new=t_new, heads=heads, npp=npp, lam_init=lam_init),
        grid_spec=grid_spec,
        out_shape=jax.ShapeDtypeStruct((bd, hq, dv), F32),
        compiler_params=_params("parallel", "arbitrary"),
    )(page_table, qs, *([ck] * npp), *([cv] * npp), k_new, v_new,
      *[x.reshape(1, dqk) for x in lams], subln.reshape(1, dv))


def _sconv_kernel(gb_ref, gc_ref, hc_ref, w_ref, hist_ref, o_ref, st_ref, ext_ref, *, tm):
    i = pl.program_id(1)

    @pl.when(i == 0)
    def _():
        ext_ref[SUBLANES - HIST:SUBLANES, :] = hist_ref[0]

    u = gc_ref[0] * hc_ref[0]
    ext_ref[SUBLANES:SUBLANES + tm, :] = u
    w = w_ref[...]
    cu = w[0:1] * ext_ref[SUBLANES - 2:SUBLANES - 2 + tm, :]
    cu = cu + w[1:2] * ext_ref[SUBLANES - 1:SUBLANES - 1 + tm, :]
    cu = cu + w[2:3] * u
    o_ref[0] = (gb_ref[0] * cu).astype(o_ref.dtype)
    last = ext_ref[tm + SUBLANES - HIST:tm + SUBLANES, :]
    st_ref[0] = last
    ext_ref[SUBLANES - HIST:SUBLANES, :] = last


def _sconv(proj3, w, hist, *, width, block0, out_dtype):
    b, s, _ = proj3.shape
    tm = _pick(s, 512, SUBLANES)
    blk = lambda c: pl.BlockSpec((1, tm, width), lambda bi, i, c=c: (bi, i, block0 + c))
    return pl.pallas_call(
        functools.partial(_sconv_kernel, tm=tm),
        grid=(b, s // tm),
        in_specs=[blk(0), blk(1), blk(2),
                  pl.BlockSpec((CONV_W, width), lambda bi, i: (0, 0)),
                  pl.BlockSpec((1, HIST, width), lambda bi, i: (bi, 0, 0))],
        out_specs=[pl.BlockSpec((1, tm, width), lambda bi, i: (bi, i, 0)),
                   pl.BlockSpec((1, HIST, width), lambda bi, i: (bi, 0, 0))],
        out_shape=[jax.ShapeDtypeStruct((b, s, width), out_dtype),
                   jax.ShapeDtypeStruct((b, HIST, width), F32)],
        scratch_shapes=[pltpu.VMEM((tm + SUBLANES, width), F32)],
        compiler_params=_params("parallel", "arbitrary"),
    )(proj3, proj3, proj3, w, hist)


def _mem_attn_kernel(q_ref, k_ref, v_ref, g_ref, o_ref, *, heads, hd):
    scale = hd ** -0.5
    for h in range(heads):
        sl = slice(h * hd, (h + 1) * hd)
        q = q_ref[0, :, sl]
        qn = q * lax.rsqrt(jnp.mean(q * q, axis=-1, keepdims=True) + RMS_EPS) * g_ref[...]
        s = lax.dot_general((qn * scale).astype(BF16), k_ref[0, :, sl].astype(BF16), (((1,), (1,)), ((), ())),
                            preferred_element_type=F32)
        p = jnp.exp(s - jnp.max(s, axis=-1, keepdims=True))
        o = jnp.dot(p.astype(BF16), v_ref[0, :, sl].astype(BF16), preferred_element_type=F32)
        o_ref[0, :, sl] = (o / jnp.sum(p, axis=-1, keepdims=True)).astype(o_ref.dtype)


def _mem_attn(q3, mk, mv, layer, g, *, heads, hd, qblock, out_dtype):
    b, s, _ = q3.shape
    n = mk.shape[2]
    width = heads * hd
    tm = _pick(s, 512, SUBLANES)
    kv = pl.BlockSpec((None, 1, n, width), lambda bi, i: (layer, bi, 0, 0))
    return pl.pallas_call(
        functools.partial(_mem_attn_kernel, heads=heads, hd=hd),
        grid=(b, s // tm),
        in_specs=[pl.BlockSpec((1, tm, width), lambda bi, i: (bi, i, qblock)), kv, kv,
                  pl.BlockSpec((1, hd), lambda bi, i: (0, 0))],
        out_specs=pl.BlockSpec((1, tm, width), lambda bi, i: (bi, i, 0)),
        out_shape=jax.ShapeDtypeStruct((b, s, width), out_dtype),
        compiler_params=_params("parallel", "parallel"),
    )(q3, mk, mv, g.reshape(1, hd))


def _head_norm_kernel(x_ref, g_ref, o_ref, *, heads, hd):
    for h in range(heads):
        sl = slice(h * hd, (h + 1) * hd)
        x = x_ref[:, sl]
        o_ref[:, sl] = x * lax.rsqrt(jnp.mean(x * x, axis=-1, keepdims=True) + RMS_EPS) * g_ref[...]


def _head_norm(x, g, *, heads, hd):
    m, width = x.shape
    tm = _pick(m, 512, SUBLANES)
    return pl.pallas_call(
        functools.partial(_head_norm_kernel, heads=heads, hd=hd),
        grid=(m // tm,),
        in_specs=[pl.BlockSpec((tm, width), lambda i: (i, 0)), pl.BlockSpec((1, hd), lambda i: (0, 0))],
        out_specs=pl.BlockSpec((tm, width), lambda i: (i, 0)),
        out_shape=jax.ShapeDtypeStruct((m, width), F32),
        compiler_params=_params("parallel"),
    )(x, g.reshape(1, hd))


def _ffn_activation(g, u, ext_ref, w_ref, b_ref, tm):
    ext_ref[SUBLANES:SUBLANES + tm, :] = g
    w = w_ref[...]
    gc = w[0:1] * ext_ref[SUBLANES - 2:SUBLANES - 2 + tm, :]
    gc = gc + w[1:2] * ext_ref[SUBLANES - 1:SUBLANES - 1 + tm, :]
    gc = gc + w[2:3] * g
    gc = gc + b_ref[...]
    return gc * (1.0 / (1.0 + jnp.exp(-gc))) * u


def _ffn_fused_kernel(x_ref, wgu_ref, cw_ref, cb_ref, hist_ref, a_ref, st_ref, tail_ref, *, s, rc):
    tn = a_ref.shape[2]
    cols = pl.ds(pl.multiple_of(pl.program_id(1) * tn, tn), tn)
    tail_ref[SUBLANES - HIST:SUBLANES, :] = hist_ref[0, :, cols]
    w = cw_ref[:, cols]
    bias = cb_ref[:, cols]
    row = lax.broadcasted_iota(jnp.int32, (SUBLANES, tn), 0)
    for c in range(s // rc):
        r0 = c * rc
        gu = jnp.dot(x_ref[0, r0:r0 + rc, :], wgu_ref[...], preferred_element_type=F32)
        g = gu[:, :tn]
        u = gu[:, tn:]
        p2 = tail_ref[SUBLANES - 2:SUBLANES - 1, :]
        p1 = tail_ref[SUBLANES - 1:SUBLANES, :]
        g1 = pltpu.roll(g, 1, axis=0)
        g2 = pltpu.roll(g, 2, axis=0)
        h1 = jnp.where(row == 0, p1, g1[0:SUBLANES])
        h2 = jnp.where(row == 0, p2, jnp.where(row == 1, p1, g2[0:SUBLANES]))
        g1 = jnp.concatenate([h1, g1[SUBLANES:]], axis=0)
        g2 = jnp.concatenate([h2, g2[SUBLANES:]], axis=0)
        tail_ref[...] = g[rc - SUBLANES:rc]
        gc = w[0:1] * g2 + w[1:2] * g1 + w[2:3] * g + bias
        a_ref[0, r0:r0 + rc, :] = (gc * (1.0 / (1.0 + jnp.exp(-gc))) * u).astype(a_ref.dtype)
    st_ref[0] = tail_ref[SUBLANES - HIST:SUBLANES, :]


def _ffn_fused(h3, wgu, cw, cb, hist, *, tn, rc=256):
    b, s, d = h3.shape
    f = wgu.shape[2] // 2
    rc = _pick(s, rc, SUBLANES)
    hspec = pl.BlockSpec((1, HIST, tn), lambda bi, j: (bi, 0, j))
    return pl.pallas_call(
        functools.partial(_ffn_fused_kernel, s=s, rc=rc),
        grid=(b, f // tn),
        in_specs=[pl.BlockSpec((1, s, d), lambda bi, j: (bi, 0, 0)),
                  pl.BlockSpec((None, d, 2 * tn), lambda bi, j: (0, 0, j)),
                  pl.BlockSpec((CONV_W, f), lambda bi, j: (0, 0)),
                  pl.BlockSpec((1, f), lambda bi, j: (0, 0)),
                  pl.BlockSpec((1, HIST, f), lambda bi, j: (bi, 0, 0))],
        out_specs=[pl.BlockSpec((1, s, tn), lambda bi, j: (bi, 0, j)), hspec],
        out_shape=[jax.ShapeDtypeStruct((b, s, f), BF16), jax.ShapeDtypeStruct((b, HIST, f), F32)],
        scratch_shapes=[pltpu.VMEM((SUBLANES, tn), F32)],
        compiler_params=_params("parallel", "parallel"),
    )(h3, wgu, cw, cb.reshape(1, f), hist)


def _ffn_act_kernel(g_ref, u_ref, cw_ref, cb_ref, hist_ref, a_ref, st_ref, ext_ref, *, tm):
    ext_ref[SUBLANES - HIST:SUBLANES, :] = hist_ref[0]
    a_ref[0] = _ffn_activation(g_ref[0], u_ref[0], ext_ref, cw_ref, cb_ref, tm).astype(a_ref.dtype)
    st_ref[0] = ext_ref[tm + SUBLANES - HIST:tm + SUBLANES, :]


def _ffn_act(g3, u3, cw, cb, hist, *, tn=1024):
    b, t, f = g3.shape
    tn = _pick(f, tn, LANES)
    blk = pl.BlockSpec((1, t, tn), lambda bi, j: (bi, 0, j))
    hspec = pl.BlockSpec((1, HIST, tn), lambda bi, j: (bi, 0, j))
    return pl.pallas_call(
        functools.partial(_ffn_act_kernel, tm=t),
        grid=(b, f // tn),
        in_specs=[blk, blk, pl.BlockSpec((CONV_W, tn), lambda bi, j: (0, j)),
                  pl.BlockSpec((1, tn), lambda bi, j: (0, j)), hspec],
        out_specs=[blk, hspec],
        out_shape=[jax.ShapeDtypeStruct((b, t, f), F32), jax.ShapeDtypeStruct((b, HIST, f), F32)],
        scratch_shapes=[pltpu.VMEM((t + SUBLANES, tn), F32)],
        compiler_params=_params("parallel", "parallel"),
    )(g3, u3, cw, cb.reshape(1, f), hist)


def kernel(x_prompt, x_sample, cache_k, cache_v, cache_mem_k, cache_mem_v, state_conv, state_ffn, page_table, mem_prompt, norm_mix, w_in, q_norm, k_norm, lambda_q1, lambda_k1, lambda_q2, lambda_k2, subln, sc_conv_w, mq_norm, norm_mem, w_mem_k, w_mem_v, mk_norm, w_out, norm_ffn, w_gate, w_up, ffn_conv_w, ffn_conv_b, w_down):
    b, s, d = x_prompt.shape
    bd, t, _ = x_sample.shape
    depth, n_pool, page, heads, _, dqk = cache_k.shape
    dv = cache_v.shape[-1]
    qcols = heads * 2 * dqk
    da_width = heads * dv
    sc_width = sc_conv_w.shape[-1]
    n_mem, mem_heads, mem_hd = cache_mem_k.shape[2:]
    mem_width = mem_heads * mem_hd
    d_ff = w_gate.shape[-1]
    n_pages = page_table.shape[1]
    past_len = n_pages * page
    c2 = 2 * qcols + da_width
    c5 = c2 + 3 * sc_width
    assert qcols == da_width and dv == 2 * dqk
    assert c2 % sc_width == 0 and c5 % mem_width == 0 and (2 * qcols) % dv == 0
    assert t <= page and (2 * heads * t) % SUBLANES == 0
    sc_block0, mq_block, v_block0 = c2 // sc_width, c5 // mem_width, (2 * qcols) // dv

    cos_p, sin_p = _rope_tables(jnp.arange(s, dtype=jnp.int32), dqk)
    cos_s, sin_s = _rope_tables(past_len + jnp.arange(t, dtype=jnp.int32), dqk)
    t_mq = 16
    assert t <= t_mq
    new_cols = -(-t * heads // LANES) * LANES
    ffn_tn = _pick(d_ff, 256, LANES)

    xp = x_prompt.reshape(b * s, d)
    xs = x_sample.reshape(bd * t, d)
    mem2 = mem_prompt.reshape(b * n_mem, d)
    zeros_conv = jnp.zeros((b, HIST, sc_width), F32)
    zeros_ffn = jnp.zeros((b, HIST, d_ff), F32)
    w_mem_k_b, w_mem_v_b = w_mem_k.astype(BF16), w_mem_v.astype(BF16)
    mem_k4 = cache_mem_k.reshape(depth, bd, n_mem, mem_width)
    mem_v4 = cache_mem_v.reshape(depth, bd, n_mem, mem_width)

    outs = {k: [] for k in ("kp", "vp", "cp", "fp", "mkp", "mvp", "ks", "vs", "cs", "fs")}
    for l in range(depth):
        lam_init = _lambda_init(l)
        lams = (lambda_q1[l], lambda_k1[l], lambda_q2[l], lambda_k2[l])

        proj, w_in_b = _matmul_cast(_rmsnorm(xs, norm_mix[l]), w_in, l)
        proj3 = proj.reshape(bd, t, -1)
        qb, kf, _ = _qk_rope(proj3, cos_s, sin_s, q_norm[l], k_norm[l], qcols=qcols, dqk=dqk, qscale=dqk ** -0.5)
        v_new = proj3[:, :, 2 * qcols:c2]
        qs = qb.reshape(bd, t, heads, 2, dqk).transpose(0, 3, 2, 1, 4).reshape(bd, 2, heads * t, dqk)
        kn = kf.reshape(bd, t, heads, 2, dqk).transpose(0, 3, 1, 2, 4).reshape(bd, 2, t * heads, dqk)
        kn = jnp.pad(kn, ((0, 0), (0, 0), (0, new_cols - t * heads), (0, 0)))
        vn = jnp.pad(v_new.reshape(bd, t * heads, dv), ((0, 0), (0, new_cols - t * heads), (0, 0)))
        da = _attn_paged(page_table, qs, cache_k, cache_v, kn, vn, lams, subln[l], layer=l, t_new=t,
                         lam_init=lam_init)
        da = da.reshape(bd, heads, t, dv).transpose(0, 2, 1, 3).reshape(bd, t, da_width)
        sc, conv_state = _sconv(proj3, sc_conv_w[l], state_conv[l], width=sc_width, block0=sc_block0, out_dtype=F32)
        mq = jnp.pad(proj3[:, :, c5:], ((0, 0), (0, t_mq - t), (0, 0)))
        mo = _mem_attn(mq, mem_k4, mem_v4, l, mq_norm[l], heads=mem_heads, hd=mem_hd, qblock=0,
                       out_dtype=F32)[:, :t]
        mix = jnp.concatenate([da, sc, mo], axis=-1).astype(BF16).reshape(bd * t, -1)
        xs, w_out_b = _matmul_cast(mix, w_out, l, xs)
        h2 = _rmsnorm(xs, norm_ffn[l])
        g2, u2, w_gu_b = _matmul_cast_pair(h2, w_gate, w_up, l, tn=ffn_tn)
        act, ffn_state = _ffn_act(g2.reshape(bd, t, d_ff), u2.reshape(bd, t, d_ff), ffn_conv_w[l], ffn_conv_b[l],
                                  state_ffn[l], tn=d_ff // 2)
        xs, w_down_b = _matmul_cast(act.astype(BF16).reshape(bd * t, d_ff), w_down, l, xs)

        outs["ks"].append(kf.reshape(bd, t, heads, 2, dqk))
        outs["vs"].append(v_new.reshape(bd, t, heads, dv))
        outs["cs"].append(conv_state)
        outs["fs"].append(ffn_state)

        hm = _rmsnorm(mem2, norm_mem[l])
        mk = _head_norm(_matmul(hm, w_mem_k_b, l), mk_norm[l], heads=mem_heads, hd=mem_hd)
        mv = _matmul(hm, w_mem_v_b, l)
        mk3 = mk.reshape(1, b, n_mem, mem_width)
        mv3 = mv.reshape(1, b, n_mem, mem_width)

        proj3 = _matmul(_rmsnorm(xp, norm_mix[l]), w_in_b, 0).reshape(b, s, -1)
        qb, kf, kb = _qk_rope(proj3, cos_p, sin_p, q_norm[l], k_norm[l], qcols=qcols, dqk=dqk,
                              qscale=dqk ** -0.5 * math.log2(math.e))
        da = _attn_prompt(qb, kb, proj3, lams, subln[l], heads=heads, dqk=dqk, dv=dv, vblock0=v_block0,
                          lam_init=lam_init)
        sc, conv_state = _sconv(proj3, sc_conv_w[l], zeros_conv, width=sc_width, block0=sc_block0, out_dtype=BF16)
        mo = _mem_attn(proj3, mk3, mv3, 0, mq_norm[l], heads=mem_heads, hd=mem_hd, qblock=mq_block, out_dtype=BF16)
        xp = _matmul([da.reshape(b * s, -1), sc.reshape(b * s, -1), mo.reshape(b * s, -1)], w_out_b, 0, xp)
        h2 = _rmsnorm(xp, norm_ffn[l]).reshape(b, s, d)
        act, ffn_state = _ffn_fused(h2, w_gu_b, ffn_conv_w[l], ffn_conv_b[l], zeros_ffn, tn=ffn_tn)
        xp = _matmul(act.reshape(b * s, d_ff), w_down_b, 0, xp, tm=512, tn=512)

        outs["kp"].append(kf.reshape(b, s, heads, 2, dqk))
        outs["vp"].append(proj3[:, :, 2 * qcols:c2].reshape(b, s, heads, dv))
        outs["cp"].append(conv_state)
        outs["fp"].append(ffn_state)
        outs["mkp"].append(mk.reshape(b, n_mem, mem_heads, mem_hd))
        outs["mvp"].append(mv.reshape(b, n_mem, mem_heads, mem_hd))

    st = lambda k: jnp.stack(outs[k])
    return (xp.reshape(b, s, d), xs.reshape(bd, t, d),
            st("kp"), st("vp"), st("cp"), st("fp"), st("mkp"), st("mvp"),
            st("ks"), st("vs"), st("cs"), st("fs"))
```
